```python
import jax
import jax.numpy as jnp
from jax import lax
import numpy as np

D_MODEL = 1024
BATCH = 8
SEQ = 4096
DEPTH = 4

N_MIXERS = 3
N_HEADS = 16
HEAD_DIM = D_MODEL // N_HEADS
CONV_WIDTH = 3
Q_BLOCK = 128
N_EXPERTS = 32
TOP_K = 4
D_EXPERT = D_MODEL
SWIGLU_LIMIT = 7.0
SWIGLU_ALPHA = 1.702
PLE_DIM = 256
EXPERT_ROW_BLOCK = 512
LN_EPS = 1e-5
DEEPNORM_ALPHA = (2 * DEPTH) ** 0.25
DEEPNORM_BETA = (8 * DEPTH) ** -0.25
FOX_GATE_BIAS = 3.0

kernel_name = 'hybrid_conv_stickbreak_fox_moe'


def layer_norm(x, g, b):
    xf = x.astype(jnp.float32)
    mu = jnp.mean(xf, axis=-1, keepdims=True)
    var = jnp.mean(jnp.square(xf - mu), axis=-1, keepdims=True)
    y = (xf - mu) * lax.rsqrt(var + LN_EPS)
    return (y * g.astype(jnp.float32) + b.astype(jnp.float32)).astype(x.dtype)


def split_heads(t):
    bsz, s, _ = t.shape
    return t.reshape(bsz, s, N_HEADS, HEAD_DIM)


def short_conv_mixer(x, w_in, w_conv, w_out):
    s = x.shape[1]
    gate_b, gate_c, h = jnp.split(x @ w_in, 3, axis=-1)
    u = jnp.pad(gate_c * h, ((0, 0), (CONV_WIDTH - 1, 0), (0, 0)))
    conv = sum(w_conv[k] * u[:, k:k + s] for k in range(CONV_WIDTH))
    return (gate_b * conv) @ w_out


def stick_breaking_mixer(x, w_in, w_out):
    bsz, s, d = x.shape
    q, k, v = (split_heads(t) for t in jnp.split(x @ w_in, 3, axis=-1))
    scale = HEAD_DIM ** -0.5
    outs = []
    for q0 in range(0, s, Q_BLOCK):
        q1 = q0 + Q_BLOCK
        z = jnp.einsum('bqhd,bkhd->bhqk', q[:, q0:q1], k[:, :q1],
                       preferred_element_type=jnp.float32) * scale
        t_pos = jnp.arange(q0, q1)[:, None]
        s_pos = jnp.arange(q1)[None, :]
        strict = s_pos < t_pos
        log_not = jnp.where(strict, -jax.nn.softplus(z), 0.0)
        between = lax.cumsum(log_not, axis=3, reverse=True) - log_not
        w = jnp.where(strict, jnp.exp(jax.nn.log_sigmoid(z) + between), 0.0)
        outs.append(jnp.einsum('bhqk,bkhd->bqhd', w.astype(v.dtype), v[:, :q1]))
    o = jnp.concatenate(outs, axis=1).reshape(bsz, s, d)
    return o @ w_out


def forgetting_attention_mixer(x, w_in, b_f, w_out):
    bsz, s, d = x.shape
    proj = x @ w_in
    q, k, v = (split_heads(t) for t in jnp.split(proj[..., :3 * d], 3, axis=-1))
    log_f = jax.nn.log_sigmoid(proj[..., 3 * d:].astype(jnp.float32) + b_f.astype(jnp.float32))
    cum = jnp.swapaxes(jnp.cumsum(log_f, axis=1), 1, 2)
    scale = HEAD_DIM ** -0.5
    outs = []
    for q0 in range(0, s, Q_BLOCK):
        q1 = q0 + Q_BLOCK
        z = jnp.einsum('bqhd,bkhd->bhqk', q[:, q0:q1], k[:, :q1],
                       preferred_element_type=jnp.float32) * scale
        z = z + cum[:, :, q0:q1, None] - cum[:, :, None, :q1]
        causal = jnp.arange(q1)[None, :] <= jnp.arange(q0, q1)[:, None]
        probs = jax.nn.softmax(jnp.where(causal, z, -jnp.inf), axis=-1)
        outs.append(jnp.einsum('bhqk,bkhd->bqhd', probs.astype(v.dtype), v[:, :q1]))
    o = jnp.concatenate(outs, axis=1).reshape(bsz, s, d)
    return o @ w_out


def clamped_swiglu(h_gate, h_up):
    h_gate = jnp.minimum(h_gate, SWIGLU_LIMIT)
    h_up = jnp.clip(h_up, -SWIGLU_LIMIT, SWIGLU_LIMIT)
    return h_gate * jax.nn.sigmoid(SWIGLU_ALPHA * h_gate) * (h_up + 1.0)


def moe_ffn(x2d, w_router, b_router, w_gate, b_gate, w_up, b_up, w_down, b_down):
    n_tok, d = x2d.shape
    logits = (x2d @ w_router).astype(jnp.float32) + b_router.astype(jnp.float32)
    top_logit, top_e = lax.top_k(logits, TOP_K)
    gate = jax.nn.softmax(top_logit, axis=-1)
    n_slots = n_tok * TOP_K
    flat_e = top_e.reshape(-1)
    flat_tok = jnp.repeat(jnp.arange(n_tok, dtype=jnp.int32), TOP_K)
    order = jnp.argsort(flat_e)
    se, stok, sgate = flat_e[order], flat_tok[order], gate.reshape(-1)[order]
    counts = jnp.bincount(flat_e, length=N_EXPERTS)
    padded = (counts + EXPERT_ROW_BLOCK - 1) // EXPERT_ROW_BLOCK * EXPERT_ROW_BLOCK
    start = jnp.cumsum(counts) - counts
    pend = jnp.cumsum(padded)
    pstart = pend - padded
    dest = pstart[se] + jnp.arange(n_slots) - start[se]
    n_blocks = -(-n_slots // EXPERT_ROW_BLOCK) + N_EXPERTS
    n_rows = n_blocks * EXPERT_ROW_BLOCK
    row_tok = jnp.zeros((n_rows,), jnp.int32).at[dest].set(stok)
    row_gate = jnp.zeros((n_rows,), jnp.float32).at[dest].set(sgate)
    block_start = jnp.arange(n_blocks) * EXPERT_ROW_BLOCK
    block_e = jnp.minimum(jnp.sum(pend[None, :] <= block_start[:, None], axis=1), N_EXPERTS - 1)

    def expert_block(args):
        tok, g, e = args
        xb = x2d[tok]
        a = clamped_swiglu(xb @ w_gate[e] + b_gate[e], xb @ w_up[e] + b_up[e])
        return (a @ w_down[e] + b_down[e]) * g[:, None].astype(x2d.dtype)

    out = lax.map(expert_block, (row_tok.reshape(n_blocks, EXPERT_ROW_BLOCK),
                                 row_gate.reshape(n_blocks, EXPERT_ROW_BLOCK), block_e))
    return jax.ops.segment_sum(out.reshape(n_rows, d), row_tok, num_segments=n_tok)


def setup_inputs(seed: int = 0) -> dict:
    key = jax.random.key(seed)
    ks = jax.random.split(key, 25)
    d, f, h, e = D_MODEL, D_EXPERT, N_HEADS, N_EXPERTS
    n_a = len(range(0, DEPTH, N_MIXERS))
    n_b = len(range(1, DEPTH, N_MIXERS))
    n_c = len(range(2, DEPTH, N_MIXERS))

    def nrm(k, shape, s):
        return jax.random.normal(k, shape, jnp.float32) * s

    return {
        'x': nrm(ks[0], (BATCH, SEQ, d), 1.0),
        'p': nrm(ks[1], (DEPTH, BATCH, SEQ, PLE_DIM), 1.0),
        'conv_w_in': nrm(ks[2], (n_a, d, 3 * d), d ** -0.5),
        'conv_w': nrm(ks[3], (n_a, CONV_WIDTH, d), CONV_WIDTH ** -0.5),
        'conv_w_out': nrm(ks[4], (n_a, d, d), d ** -0.5 * DEEPNORM_BETA),
        'sb_w_in': nrm(ks[5], (n_b, d, 3 * d), d ** -0.5),
        'sb_w_out': nrm(ks[6], (n_b, d, d), d ** -0.5 * DEEPNORM_BETA),
        'fox_w_in': nrm(ks[7], (n_c, d, 3 * d + h), d ** -0.5),
        'fox_b_f': FOX_GATE_BIAS + nrm(ks[8], (n_c, h), 0.5),
        'fox_w_out': nrm(ks[9], (n_c, d, d), d ** -0.5 * DEEPNORM_BETA),
        'ln1_g': 1.0 + nrm(ks[10], (DEPTH, d), 0.02),
        'ln1_b': nrm(ks[11], (DEPTH, d), 0.02),
        'ln2_g': 1.0 + nrm(ks[12], (DEPTH, d), 0.02),
        'ln2_b': nrm(ks[13], (DEPTH, d), 0.02),
        'router_w': nrm(ks[14], (DEPTH, d, e), d ** -0.5),
        'router_b': nrm(ks[15], (DEPTH, e), 0.01),
        'exp_w_gate': nrm(ks[16], (DEPTH, e, d, f), d ** -0.5),
        'exp_b_gate': nrm(ks[17], (DEPTH, e, f), 0.01),
        'exp_w_up': nrm(ks[18], (DEPTH, e, d, f), d ** -0.5),
        'exp_b_up': nrm(ks[19], (DEPTH, e, f), 0.01),
        'exp_w_down': nrm(ks[20], (DEPTH, e, f, d), f ** -0.5 * DEEPNORM_BETA),
        'exp_b_down': nrm(ks[21], (DEPTH, e, d), 0.01),
        'ple_w_proj': nrm(ks[22], (DEPTH, PLE_DIM, d), PLE_DIM ** -0.5 * DEEPNORM_BETA),
        'ple_w_gate': nrm(ks[23], (DEPTH, d, d), d ** -0.5),
        'ple_b_gate': nrm(ks[24], (DEPTH, d), 0.01),
    }


def reference(x, p, conv_w_in, conv_w, conv_w_out, sb_w_in, sb_w_out, fox_w_in, fox_b_f, fox_w_out,
              ln1_g, ln1_b, ln2_g, ln2_b, router_w, router_b, exp_w_gate, exp_b_gate, exp_w_up,
              exp_b_up, exp_w_down, exp_b_down, ple_w_proj, ple_w_gate, ple_b_gate):
    bsz, s, d = x.shape
    for i in range(DEPTH):
        kind, j = i % N_MIXERS, i // N_MIXERS
        if kind == 0:
            mix = short_conv_mixer(x, conv_w_in[j], conv_w[j], conv_w_out[j])
        elif kind == 1:
            mix = stick_breaking_mixer(x, sb_w_in[j], sb_w_out[j])
        else:
            mix = forgetting_attention_mixer(x, fox_w_in[j], fox_b_f[j], fox_w_out[j])
        x = layer_norm(DEEPNORM_ALPHA * x + mix, ln1_g[i], ln1_b[i])
        ffn = moe_ffn(x.reshape(bsz * s, d), router_w[i], router_b[i], exp_w_gate[i], exp_b_gate[i],
                      exp_w_up[i], exp_b_up[i], exp_w_down[i], exp_b_down[i]).reshape(bsz, s, d)
        x = layer_norm(DEEPNORM_ALPHA * x + ffn, ln2_g[i], ln2_b[i])
        x = x + (p[i] @ ple_w_proj[i]) * jax.nn.sigmoid(x @ ple_w_gate[i] + ple_b_gate[i])
    return x
```

```python
import functools

import jax
import jax.numpy as jnp
from jax import lax
from jax.experimental import pallas as pl
from jax.experimental.pallas import tpu as pltpu

F32 = jnp.float32
BF16 = jnp.bfloat16
I32 = jnp.int32

N_HEADS = 16
HEAD_DIM = 64
N_MIXERS = 3
N_EXPERTS = 32
TOP_K = 4
SWIGLU_LIMIT = 7.0
SWIGLU_ALPHA = 1.702
LN_EPS = 1e-5
LANES = 128
VMEM_LIMIT = 48 * 1024 * 1024

ROW_TILE = 512
ATT_TILE = 256
ROUTE_TILE = 512
EXPERT_TILE = 512
MOVE_TILE = 1024
COMBINE_TILE = 256

_NT = (((1,), (1,)), ((), ()))


def _params(*sem):
    return pltpu.CompilerParams(dimension_semantics=sem, vmem_limit_bytes=VMEM_LIMIT)


def _bdot(a, b):
    return jnp.dot(a, b, preferred_element_type=F32)


def _split3(v):
    hi = v.astype(BF16)
    r = v - hi.astype(F32)
    mid = r.astype(BF16)
    lo = (r - mid.astype(F32)).astype(BF16)
    return hi, mid, lo


def _softplus(z):
    return jnp.maximum(z, 0.0) + jnp.log1p(jnp.exp(-jnp.abs(z)))


def _layer_norm(h, g, b):
    mu = jnp.mean(h, axis=-1, keepdims=True)
    c = h - mu
    var = jnp.mean(c * c, axis=-1, keepdims=True)
    return c * lax.rsqrt(var + LN_EPS) * g + b


def _proj_kernel(x_ref, w_ref, o_ref):
    o_ref[...] = _bdot(x_ref[...].astype(BF16), w_ref[...]).astype(o_ref.dtype)


def _project(x2d, w_bf16, out_dtype):
    t, d = x2d.shape
    n = w_bf16.shape[1]
    return pl.pallas_call(
        _proj_kernel,
        grid=(t // ROW_TILE,),
        in_specs=[pl.BlockSpec((ROW_TILE, d), lambda i: (i, 0)),
                  pl.BlockSpec((d, n), lambda i: (0, 0))],
        out_specs=pl.BlockSpec((ROW_TILE, n), lambda i: (i, 0)),
        out_shape=jax.ShapeDtypeStruct((t, n), out_dtype),
        compiler_params=_params("parallel"),
    )(x2d, w_bf16)


def _fox_proj_kernel(x_ref, w_ref, wf_ref, bf_ref, qkv_ref, cum_ref, carry_ref, *, tiles_per_seq):
    i = pl.program_id(0)

    @pl.when(i % tiles_per_seq == 0)
    def _():
        carry_ref[...] = jnp.zeros_like(carry_ref)

    xb = x_ref[...].astype(BF16)
    qkv_ref[...] = _bdot(xb, w_ref[...]).astype(qkv_ref.dtype)
    f = _bdot(xb, wf_ref[...]) + bf_ref[...]
    log_f = jnp.minimum(f, 0.0) - jnp.log1p(jnp.exp(-jnp.abs(f)))
    tm = f.shape[0]
    r = lax.broadcasted_iota(I32, (tm, tm), 0)
    c = lax.broadcasted_iota(I32, (tm, tm), 1)
    tri = jnp.where(c <= r, 1.0, 0.0).astype(BF16)
    hi, mid, lo = _split3(log_f)
    cum = _bdot(tri, hi) + _bdot(tri, mid) + _bdot(tri, lo) + carry_ref[...]
    cum_ref[...] = cum
    carry_ref[...] = cum[tm - 1:tm, :]


def _fox_project(x2d, w_qkv, w_f, b_f, seq):
    t, d = x2d.shape
    n = w_qkv.shape[1]
    return pl.pallas_call(
        functools.partial(_fox_proj_kernel, tiles_per_seq=seq // ROW_TILE),
        grid=(t // ROW_TILE,),
        in_specs=[pl.BlockSpec((ROW_TILE, d), lambda i: (i, 0)),
                  pl.BlockSpec((d, n), lambda i: (0, 0)),
                  pl.BlockSpec((d, LANES), lambda i: (0, 0)),
                  pl.BlockSpec((1, LANES), lambda i: (0, 0))],
        out_specs=[pl.BlockSpec((ROW_TILE, n), lambda i: (i, 0)),
                   pl.BlockSpec((ROW_TILE, LANES), lambda i: (i, 0))],
        out_shape=[jax.ShapeDtypeStruct((t, n), BF16),
                   jax.ShapeDtypeStruct((t, LANES), F32)],
        scratch_shapes=[pltpu.VMEM((1, LANES), F32)],
        compiler_params=_params("arbitrary"),
    )(x2d, w_qkv, w_f, b_f)


def _conv_kernel(x_ref, w_ref, cw_ref, o_ref, tail_ref, *, tiles_per_seq):
    i = pl.program_id(0)

    @pl.when(i % tiles_per_seq == 0)
    def _():
        tail_ref[...] = jnp.zeros_like(tail_ref)

    d = o_ref.shape[1]
    proj = _bdot(x_ref[...].astype(BF16), w_ref[...])
    gate_b = proj[:, :d]
    u = proj[:, d:2 * d] * proj[:, 2 * d:]
    tm = u.shape[0]
    row = lax.broadcasted_iota(I32, u.shape, 0)
    prev1 = jnp.where(row == 0, tail_ref[1:2, :], pltpu.roll(u, 1, 0))
    prev2 = jnp.where(row == 0, tail_ref[0:1, :],
                      jnp.where(row == 1, tail_ref[1:2, :], pltpu.roll(u, 2, 0)))
    conv = cw_ref[0:1, :] * prev2 + cw_ref[1:2, :] * prev1 + cw_ref[2:3, :] * u
    o_ref[...] = (gate_b * conv).astype(o_ref.dtype)
    tail_ref[...] = u[tm - 2:tm, :]


def _conv_mix(x2d, w_in, conv_w, seq):
    t, d = x2d.shape
    return pl.pallas_call(
        functools.partial(_conv_kernel, tiles_per_seq=seq // ROW_TILE),
        grid=(t // ROW_TILE,),
        in_specs=[pl.BlockSpec((ROW_TILE, d), lambda i: (i, 0)),
                  pl.BlockSpec((d, 3 * d), lambda i: (0, 0)),
                  pl.BlockSpec(conv_w.shape, lambda i: (0, 0))],
        out_specs=pl.BlockSpec((ROW_TILE, d), lambda i: (i, 0)),
        out_shape=jax.ShapeDtypeStruct((t, d), BF16),
        scratch_shapes=[pltpu.VMEM((2, d), F32)],
        compiler_params=_params("arbitrary"),
    )(x2d, w_in, conv_w)


def _sb_kernel(q_ref, k_ref, v_ref, o_ref):
    tq = q_ref.shape[1]
    tk = tq
    qi = pl.program_id(2)
    q = q_ref[0] * jnp.asarray(HEAD_DIM ** -0.5, BF16)
    row = lax.broadcasted_iota(I32, (tq, tk), 0)
    col = lax.broadcasted_iota(I32, (tq, tk), 1)
    later = jnp.where(row > col, 1.0, 0.0).astype(BF16)

    outs = []
    for h in range(LANES // HEAD_DIM):
        lanes = slice(h * HEAD_DIM, (h + 1) * HEAD_DIM)
        qh = q[:, lanes]

        def body(j, carry, lanes=lanes, qh=qh):
            acc, run = carry
            kb = qi - j
            start = pl.multiple_of(kb * tk, tk)
            kh = k_ref[0, pl.ds(start, tk), lanes]
            vh = v_ref[0, pl.ds(start, tk), lanes]
            z = lax.dot_general(qh, kh, _NT, preferred_element_type=F32)
            sp = _softplus(z)
            strict = (col + kb * tk) < (row + qi * tq)
            log_not = jnp.where(strict, -sp, 0.0)
            hi, mid, lo = _split3(log_not)
            inner = _bdot(hi, later) + _bdot(mid, later) + _bdot(lo, later)
            w = jnp.where(strict, jnp.exp((z - sp) + (inner + run)), 0.0)
            acc = acc + _bdot(w.astype(BF16), vh)
            run = run + inner[:, 0:1] + log_not[:, 0:1]
            return acc, run

        acc, _ = lax.fori_loop(0, qi + 1, body,
                               (jnp.zeros((tq, HEAD_DIM), F32), jnp.zeros((tq, 1), F32)))
        outs.append(acc)
    o_ref[0] = jnp.concatenate(outs, axis=1).astype(o_ref.dtype)


def _fox_kernel(q_ref, k_ref, v_ref, cq_ref, ck_ref, o_ref):
    tq = q_ref.shape[1]
    tk = tq
    qi = pl.program_id(2)
    hp = pl.program_id(1)
    q = q_ref[0] * jnp.asarray(HEAD_DIM ** -0.5, BF16)
    row = lax.broadcasted_iota(I32, (tq, tk), 0)
    col = lax.broadcasted_iota(I32, (tq, tk), 1)
    cq_all = cq_ref[0]
    head_lane = lax.broadcasted_iota(I32, cq_all.shape, 1)

    outs = []
    for h in range(LANES // HEAD_DIM):
        lanes = slice(h * HEAD_DIM, (h + 1) * HEAD_DIM)
        qh = q[:, lanes]
        head = hp * (LANES // HEAD_DIM) + h
        cq = jnp.sum(jnp.where(head_lane == head, cq_all, 0.0), axis=1, keepdims=True)

        def body(kb, carry, lanes=lanes, qh=qh, cq=cq, head=head):
            acc, m, l = carry
            start = pl.multiple_of(kb * tk, tk)
            kh = k_ref[0, pl.ds(start, tk), lanes]
            vh = v_ref[0, pl.ds(start, tk), lanes]
            ck = ck_ref[0, pl.ds(head, 1), pl.ds(start, tk)]
            z = lax.dot_general(qh, kh, _NT, preferred_element_type=F32) + (cq - ck)
            causal = (col + kb * tk) <= (row + qi * tq)
            z = jnp.where(causal, z, -jnp.inf)
            m_new = jnp.maximum(m, jnp.max(z, axis=1, keepdims=True))
            alpha = jnp.exp(m - m_new)
            p = jnp.exp(z - m_new)
            l = l * alpha + jnp.sum(p, axis=1, keepdims=True)
            acc = acc * alpha + _bdot(p.astype(BF16), vh)
            return acc, m_new, l

        acc, _, l = lax.fori_loop(
            0, qi + 1, body,
            (jnp.zeros((tq, HEAD_DIM), F32), jnp.full((tq, 1), -jnp.inf, F32), jnp.zeros((tq, 1), F32)))
        outs.append(acc / l)
    o_ref[0] = jnp.concatenate(outs, axis=1).astype(o_ref.dtype)


def _attention(kernel_fn, qkv, extra, extra_specs):
    b, s, d3 = qkv.shape
    d = d3 // 3
    pairs = d // LANES
    qspec = pl.BlockSpec((1, ATT_TILE, LANES), lambda bi, hi, qi: (bi, qi, hi))
    kspec = pl.BlockSpec((1, s, LANES), lambda bi, hi, qi: (bi, 0, pairs + hi))
    vspec = pl.BlockSpec((1, s, LANES), lambda bi, hi, qi: (bi, 0, 2 * pairs + hi))
    return pl.pallas_call(
        kernel_fn,
        grid=(b, pairs, s // ATT_TILE),
        in_specs=[qspec, kspec, vspec] + extra_specs,
        out_specs=pl.BlockSpec((1, ATT_TILE, LANES), lambda bi, hi, qi: (bi, qi, hi)),
        out_shape=jax.ShapeDtypeStruct((b, s, d), BF16),
        compiler_params=_params("parallel", "parallel", "arbitrary"),
    )(qkv, qkv, qkv, *extra)


def _out_ln_kernel(o_ref, x_ref, w_ref, g_ref, b_ref, y_ref, *, alpha):
    h = alpha * x_ref[...] + _bdot(o_ref[...], w_ref[...])
    y_ref[...] = _layer_norm(h, g_ref[...], b_ref[...])


def _out_ln(o2d, x2d, w_out, g, b, alpha):
    t, d = x2d.shape
    row = lambda i: (i, 0)
    fixed = lambda i: (0, 0)
    return pl.pallas_call(
        functools.partial(_out_ln_kernel, alpha=alpha),
        grid=(t // ROW_TILE,),
        in_specs=[pl.BlockSpec((ROW_TILE, d), row), pl.BlockSpec((ROW_TILE, d), row),
                  pl.BlockSpec((d, d), fixed), pl.BlockSpec((1, d), fixed), pl.BlockSpec((1, d), fixed)],
        out_specs=pl.BlockSpec((ROW_TILE, d), row),
        out_shape=jax.ShapeDtypeStruct((t, d), F32),
        compiler_params=_params("parallel"),
    )(o2d, x2d, w_out, g, b)


def _router_kernel(x_ref, wh_ref, wl_ref, b_ref, e_ref, g_ref, r_ref, cnt_ref, carry_ref):
    i = pl.program_id(0)

    @pl.when(i == 0)
    def _():
        carry_ref[...] = jnp.zeros_like(carry_ref)

    x = x_ref[...]
    xh = x.astype(BF16)
    xl = (x - xh.astype(F32)).astype(BF16)
    wh = wh_ref[...]
    nt = functools.partial(lax.dot_general, dimension_numbers=_NT, preferred_element_type=F32)
    logits = nt(wh, xh) + nt(wl_ref[...], xh) + nt(wh, xl) + b_ref[:, 0:1]
    n_e, tm = logits.shape
    eid = lax.broadcasted_iota(I32, (n_e, tm), 0)

    work = logits
    tops, ids, sels = [], [], []
    for _ in range(TOP_K):
        m = jnp.max(work, axis=0, keepdims=True)
        idx = jnp.min(jnp.where(work == m, eid, n_e), axis=0, keepdims=True)
        sel = eid == idx
        work = jnp.where(sel, -jnp.inf, work)
        tops.append(m)
        ids.append(idx)
        sels.append(sel)
    ex = [jnp.exp(m - tops[0]) for m in tops]
    denom = ex[0] + ex[1] + ex[2] + ex[3]
    e_ref[...] = jnp.concatenate(ids, axis=0)
    g_ref[...] = jnp.concatenate([v / denom for v in ex], axis=0)

    chosen = jnp.where(sels[0] | sels[1] | sels[2] | sels[3], 1.0, 0.0)
    r = lax.broadcasted_iota(I32, (tm, tm), 0)
    c = lax.broadcasted_iota(I32, (tm, tm), 1)
    before = jnp.where(r < c, 1.0, 0.0).astype(BF16)
    seen = _bdot(chosen.astype(BF16), before) + carry_ref[:, 0:1]
    r_ref[...] = jnp.concatenate(
        [jnp.sum(jnp.where(s, seen, 0.0), axis=0, keepdims=True) for s in sels], axis=0).astype(I32)
    total = carry_ref[...] + jnp.sum(chosen, axis=1, keepdims=True)
    carry_ref[...] = total
    cnt_ref[...] = total


def _route(x2d, w_hi, w_lo, bias):
    t, d = x2d.shape
    n_e = w_hi.shape[0]
    tok = lambda i: (0, i)
    fixed = lambda i: (0, 0)
    return pl.pallas_call(
        _router_kernel,
        grid=(t // ROUTE_TILE,),
        in_specs=[pl.BlockSpec((ROUTE_TILE, d), lambda i: (i, 0)),
                  pl.BlockSpec((n_e, d), fixed), pl.BlockSpec((n_e, d), fixed),
                  pl.BlockSpec((n_e, LANES), fixed)],
        out_specs=[pl.BlockSpec((TOP_K, ROUTE_TILE), tok), pl.BlockSpec((TOP_K, ROUTE_TILE), tok),
                   pl.BlockSpec((TOP_K, ROUTE_TILE), tok), pl.BlockSpec((n_e, LANES), fixed)],
        out_shape=[jax.ShapeDtypeStruct((TOP_K, t), I32), jax.ShapeDtypeStruct((TOP_K, t), F32),
                   jax.ShapeDtypeStruct((TOP_K, t), I32), jax.ShapeDtypeStruct((n_e, LANES), F32)],
        scratch_shapes=[pltpu.VMEM((n_e, LANES), F32)],
        compiler_params=_params("arbitrary"),
    )(x2d, w_hi, w_lo, bias)


def _dest_kernel(pstart_ref, e_ref, r_ref, d_ref):
    e = e_ref[...]
    base = jnp.zeros(e.shape, I32)
    for j in range(N_EXPERTS):
        base = jnp.where(e == j, pstart_ref[j], base)
    d_ref[...] = base + r_ref[...]


def _destinations(pstart, top_e, rank):
    k, t = top_e.shape
    blk = min(t, 4096)
    spec = pl.BlockSpec((k, blk), lambda i: (0, i))
    return pl.pallas_call(
        _dest_kernel,
        grid=(t // blk,),
        in_specs=[pl.BlockSpec(memory_space=pltpu.SMEM), spec, spec],
        out_specs=spec,
        out_shape=jax.ShapeDtypeStruct((k, t), I32),
        compiler_params=_params("parallel"),
    )(pstart, top_e, rank)


def _scatter_kernel(dest_ref, x_hbm, init_hbm, xs_hbm, sem, *, tile):
    del init_hbm
    base = pl.program_id(0) * tile

    def issue(t, _):
        for k in range(TOP_K):
            pltpu.make_async_copy(x_hbm.at[pl.ds(base + t, 1)],
                                  xs_hbm.at[pl.ds(dest_ref[k, t], 1)], sem).start()
        return 0

    lax.fori_loop(0, tile, issue, 0)
    rows = xs_hbm.at[pl.ds(0, TOP_K * tile)]
    pltpu.make_async_copy(rows, rows, sem).wait()


def _dispatch(x2d, dest, n_rows):
    t, d = x2d.shape
    tile = min(MOVE_TILE, t)
    return pl.pallas_call(
        functools.partial(_scatter_kernel, tile=tile),
        grid=(t // tile,),
        in_specs=[pl.BlockSpec((TOP_K, tile), lambda i: (0, i), memory_space=pltpu.SMEM),
                  pl.BlockSpec(memory_space=pl.ANY), pl.BlockSpec(memory_space=pl.ANY)],
        out_specs=pl.BlockSpec(memory_space=pl.ANY),
        out_shape=jax.ShapeDtypeStruct((n_rows, d), x2d.dtype),
        scratch_shapes=[pltpu.SemaphoreType.DMA(())],
        input_output_aliases={2: 0},
        compiler_params=_params("arbitrary"),
    )(dest, x2d, jnp.zeros((n_rows, d), x2d.dtype))


def _expert_kernel(be_ref, nu_ref, xs_ref, wg_ref, bg_ref, wu_ref, bu_ref, wd_ref, bd_ref, y_ref):
    i = pl.program_id(0)

    @pl.when(i < nu_ref[0])
    def _():
        xb = xs_ref[...].astype(BF16)
        hg = jnp.minimum(_bdot(xb, wg_ref[0]) + bg_ref[0], SWIGLU_LIMIT)
        hu = jnp.clip(_bdot(xb, wu_ref[0]) + bu_ref[0], -SWIGLU_LIMIT, SWIGLU_LIMIT)
        a = hg * jax.nn.sigmoid(SWIGLU_ALPHA * hg) * (hu + 1.0)
        y_ref[...] = _bdot(a.astype(BF16), wd_ref[0]) + bd_ref[0]

    @pl.when(i >= nu_ref[0])
    def _():
        y_ref[...] = jnp.zeros_like(y_ref)


def _experts(block_e, n_used, xs, w_gate, b_gate, w_up, b_up, w_down, b_down):
    n_rows, d = xs.shape
    f = w_gate.shape[2]
    rows = lambda i, be, nu: (i, 0)
    wsel = lambda i, be, nu: (be[i], 0, 0)
    grid_spec = pltpu.PrefetchScalarGridSpec(
        num_scalar_prefetch=2,
        grid=(n_rows // EXPERT_TILE,),
        in_specs=[pl.BlockSpec((EXPERT_TILE, d), rows),
                  pl.BlockSpec((1, d, f), wsel), pl.BlockSpec((1, 1, f), wsel),
                  pl.BlockSpec((1, d, f), wsel), pl.BlockSpec((1, 1, f), wsel),
                  pl.BlockSpec((1, f, d), wsel), pl.BlockSpec((1, 1, d), wsel)],
        out_specs=pl.BlockSpec((EXPERT_TILE, d), rows),
    )
    return pl.pallas_call(
        _expert_kernel,
        grid_spec=grid_spec,
        out_shape=jax.ShapeDtypeStruct((n_rows, d), F32),
        compiler_params=_params("arbitrary"),
    )(block_e, n_used, xs, w_gate, b_gate, w_up, b_up, w_down, b_down)


def _combine_kernel(dest_ref, y_hbm, x_ref, g_ref, p_ref, wp_ref, wg_ref, bg_ref, lg_ref, lb_ref,
                    o_ref, buf, sem, *, alpha):
    tm = x_ref.shape[0]

    def issue(t, _):
        for k in range(TOP_K):
            pltpu.make_async_copy(y_hbm.at[pl.ds(dest_ref[k, t], 1)],
                                  buf.at[k, pl.ds(t, 1)], sem).start()
        return 0

    lax.fori_loop(0, tm, issue, 0)

    gates = jnp.concatenate([g_ref[...], jnp.zeros((LANES - TOP_K, tm), F32)], axis=0).T
    embed = _bdot(p_ref[...].astype(BF16), wp_ref[...])

    pltpu.make_async_copy(buf, buf, sem).wait()
    ffn = buf[0] * gates[:, 0:1]
    for k in range(1, TOP_K):
        ffn = ffn + buf[k] * gates[:, k:k + 1]
    x2 = _layer_norm(alpha * x_ref[...] + ffn, lg_ref[...], lb_ref[...])
    gate = jax.nn.sigmoid(_bdot(x2.astype(BF16), wg_ref[...]) + bg_ref[...])
    o_ref[...] = x2 + embed * gate


def _combine(dest, y, x2d, gates, p2d, w_proj, w_gate, b_gate, ln_g, ln_b, alpha):
    t, d = x2d.shape
    pd = p2d.shape[1]
    tile = min(COMBINE_TILE, t)
    row = lambda i: (i, 0)
    tok = lambda i: (0, i)
    fixed = lambda i: (0, 0)
    return pl.pallas_call(
        functools.partial(_combine_kernel, alpha=alpha),
        grid=(t // tile,),
        in_specs=[pl.BlockSpec((TOP_K, tile), tok, memory_space=pltpu.SMEM),
                  pl.BlockSpec(memory_space=pl.ANY),
                  pl.BlockSpec((tile, d), row),
                  pl.BlockSpec((TOP_K, tile), tok),
                  pl.BlockSpec((tile, pd), row),
                  pl.BlockSpec((pd, d), fixed), pl.BlockSpec((d, d), fixed),
                  pl.BlockSpec((1, d), fixed), pl.BlockSpec((1, d), fixed), pl.BlockSpec((1, d), fixed)],
        out_specs=pl.BlockSpec((tile, d), row),
        out_shape=jax.ShapeDtypeStruct((t, d), F32),
        scratch_shapes=[pltpu.VMEM((TOP_K, tile, d), F32), pltpu.SemaphoreType.DMA(())],
        compiler_params=_params("arbitrary"),
    )(dest, y, x2d, gates, p2d, w_proj, w_gate, b_gate, ln_g, ln_b)


def _moe(x2d, router_w, router_b, w_gate, b_gate, w_up, b_up, w_down, b_down):
    t, d = x2d.shape
    w_t = router_w.T
    w_hi = w_t.astype(BF16)
    w_lo = (w_t - w_hi.astype(F32)).astype(BF16)
    bias = jnp.broadcast_to(router_b[:, None], (N_EXPERTS, LANES))
    top_e, gates, rank, counts = _route(x2d, w_hi, w_lo, bias)

    n_blocks = -(-t * TOP_K // EXPERT_TILE) + N_EXPERTS
    cnt = counts[:, 0].astype(I32)
    padded = (cnt + EXPERT_TILE - 1) // EXPERT_TILE * EXPERT_TILE
    pend = jnp.cumsum(padded)
    pstart = pend - padded
    block_start = jnp.arange(n_blocks, dtype=I32) * EXPERT_TILE
    block_e = jnp.minimum(jnp.sum(pend[None, :] <= block_start[:, None], axis=1), N_EXPERTS - 1).astype(I32)
    n_used = (pend[-1:] // EXPERT_TILE).astype(I32)

    dest = _destinations(pstart.astype(I32), top_e, rank)
    xs = _dispatch(x2d, dest, n_blocks * EXPERT_TILE)
    y = _experts(block_e, n_used, xs, w_gate.astype(BF16), b_gate[:, None, :], w_up.astype(BF16),
                 b_up[:, None, :], w_down.astype(BF16), b_down[:, None, :])
    return dest, y, gates


def kernel(x, p, conv_w_in, conv_w, conv_w_out, sb_w_in, sb_w_out, fox_w_in, fox_b_f, fox_w_out, ln1_g, ln1_b, ln2_g, ln2_b, router_w, router_b, exp_w_gate, exp_b_gate, exp_w_up, exp_b_up, exp_w_down, exp_b_down, ple_w_proj, ple_w_gate, ple_b_gate):
    bsz, s, d = x.shape
    depth = p.shape[0]
    t = bsz * s
    alpha = float((2 * depth) ** 0.25)
    assert d == N_HEADS * HEAD_DIM and s % ROW_TILE == 0 and s % ATT_TILE == 0
    assert t % ROUTE_TILE == 0 and t % min(MOVE_TILE, t) == 0 and t % min(COMBINE_TILE, t) == 0

    xf = x.reshape(t, d)
    for i in range(depth):
        kind, j = i % N_MIXERS, i // N_MIXERS
        if kind == 0:
            mixed = _conv_mix(xf, conv_w_in[j].astype(BF16), conv_w[j], s)
            w_out = conv_w_out[j]
        elif kind == 1:
            qkv = _project(xf, sb_w_in[j].astype(BF16), BF16).reshape(bsz, s, 3 * d)
            mixed = _attention(_sb_kernel, qkv, [], []).reshape(t, d)
            w_out = sb_w_out[j]
        else:
            w_in = fox_w_in[j]
            w_f = jnp.pad(w_in[:, 3 * d:], ((0, 0), (0, LANES - N_HEADS))).astype(BF16)
            b_f = jnp.pad(fox_b_f[j], (0, LANES - N_HEADS))[None, :]
            qkv, cum = _fox_project(xf, w_in[:, :3 * d].astype(BF16), w_f, b_f, s)
            cum = cum[:, :N_HEADS].reshape(bsz, s, N_HEADS)
            cum_t = jnp.swapaxes(cum, 1, 2)
            mixed = _attention(
                _fox_kernel, qkv.reshape(bsz, s, 3 * d), [cum, cum_t],
                [pl.BlockSpec((1, ATT_TILE, N_HEADS), lambda bi, hi, qi: (bi, qi, 0)),
                 pl.BlockSpec((1, N_HEADS, s), lambda bi, hi, qi: (bi, 0, 0))]).reshape(t, d)
            w_out = fox_w_out[j]
        x1 = _out_ln(mixed, xf, w_out.astype(BF16), ln1_g[i][None, :], ln1_b[i][None, :], alpha)
        dest, y, gates = _moe(x1, router_w[i], router_b[i], exp_w_gate[i], exp_b_gate[i], exp_w_up[i],
                              exp_b_up[i], exp_w_down[i], exp_b_down[i])
        xf = _combine(dest, y, x1, gates, p[i].reshape(t, -1), ple_w_proj[i].astype(BF16),
                      ple_w_gate[i].astype(BF16), ple_b_gate[i][None, :], ln2_g[i][None, :],
                      ln2_b[i][None, :], alpha)
    return xf.reshape(bsz, s, d)
```

```python
import functools

import jax
import jax.numpy as jnp
from jax import lax
from jax.experimental import pallas as pl
from jax.experimental.pallas import tpu as pltpu

F32 = jnp.float32
BF16 = jnp.bfloat16
I32 = jnp.int32

N_HEADS = 16
HEAD_DIM = 64
N_MIXERS = 3
N_EXPERTS = 32
TOP_K = 4
SWIGLU_LIMIT = 7.0
SWIGLU_ALPHA = 1.702
LN_EPS = 1e-5
LANES = 128
VMEM_LIMIT = 48 * 1024 * 1024
EXPERT_VMEM_LIMIT = 58 * 1024 * 1024

ROW_TILE = 512
ATT_Q = 256
ATT_K = 256
ATT_UNROLL = 4
ROUTE_TILE = 512
EXPERT_TILE = 512
MOVE_TILE = 1024
COMBINE_TILE = 256
ISSUE_UNROLL = 4

_NT = (((1,), (1,)), ((), ()))


def _params(*sem):
    return pltpu.CompilerParams(dimension_semantics=sem, vmem_limit_bytes=VMEM_LIMIT)


def _bdot(a, b):
    return jnp.dot(a, b, preferred_element_type=F32)


def _split3(v):
    hi = v.astype(BF16)
    r = v - hi.astype(F32)
    mid = r.astype(BF16)
    lo = (r - mid.astype(F32)).astype(BF16)
    return hi, mid, lo


def _layer_norm(h, g, b):
    mu = jnp.mean(h, axis=-1, keepdims=True)
    c = h - mu
    var = jnp.mean(c * c, axis=-1, keepdims=True)
    return c * lax.rsqrt(var + LN_EPS) * g + b


def _proj_kernel(x_ref, w_ref, o_ref):
    o_ref[...] = _bdot(x_ref[...].astype(BF16), w_ref[...]).astype(o_ref.dtype)


def _project(x2d, w_bf16, out_dtype):
    t, d = x2d.shape
    n = w_bf16.shape[1]
    return pl.pallas_call(
        _proj_kernel,
        grid=(t // ROW_TILE,),
        in_specs=[pl.BlockSpec((ROW_TILE, d), lambda i: (i, 0)),
                  pl.BlockSpec((d, n), lambda i: (0, 0))],
        out_specs=pl.BlockSpec((ROW_TILE, n), lambda i: (i, 0)),
        out_shape=jax.ShapeDtypeStruct((t, n), out_dtype),
        compiler_params=_params("parallel"),
    )(x2d, w_bf16)


def _fox_proj_kernel(x_ref, w_ref, wf_ref, bf_ref, qkv_ref, cum_ref, carry_ref, *, tiles_per_seq):
    i = pl.program_id(0)

    @pl.when(i % tiles_per_seq == 0)
    def _():
        carry_ref[...] = jnp.zeros_like(carry_ref)

    xb = x_ref[...].astype(BF16)
    qkv_ref[...] = _bdot(xb, w_ref[...]).astype(qkv_ref.dtype)
    f = _bdot(xb, wf_ref[...]) + bf_ref[...]
    log_f = jnp.minimum(f, 0.0) - jnp.log1p(jnp.exp(-jnp.abs(f)))
    tm = f.shape[0]
    r = lax.broadcasted_iota(I32, (tm, tm), 0)
    c = lax.broadcasted_iota(I32, (tm, tm), 1)
    tri = jnp.where(c <= r, 1.0, 0.0).astype(BF16)
    hi, mid, lo = _split3(log_f)
    cum = _bdot(tri, hi) + _bdot(tri, mid) + _bdot(tri, lo) + carry_ref[...]
    cum_ref[...] = cum
    carry_ref[...] = cum[tm - 1:tm, :]


def _fox_project(x2d, w_qkv, w_f, b_f, seq):
    t, d = x2d.shape
    n = w_qkv.shape[1]
    return pl.pallas_call(
        functools.partial(_fox_proj_kernel, tiles_per_seq=seq // ROW_TILE),
        grid=(t // ROW_TILE,),
        in_specs=[pl.BlockSpec((ROW_TILE, d), lambda i: (i, 0)),
                  pl.BlockSpec((d, n), lambda i: (0, 0)),
                  pl.BlockSpec((d, LANES), lambda i: (0, 0)),
                  pl.BlockSpec((1, LANES), lambda i: (0, 0))],
        out_specs=[pl.BlockSpec((ROW_TILE, n), lambda i: (i, 0)),
                   pl.BlockSpec((ROW_TILE, LANES), lambda i: (i, 0))],
        out_shape=[jax.ShapeDtypeStruct((t, n), BF16),
                   jax.ShapeDtypeStruct((t, LANES), F32)],
        scratch_shapes=[pltpu.VMEM((1, LANES), F32)],
        compiler_params=_params("arbitrary"),
    )(x2d, w_qkv, w_f, b_f)


def _conv_kernel(x_ref, w_ref, cw_ref, o_ref, tail_ref, *, tiles_per_seq):
    i = pl.program_id(0)

    @pl.when(i % tiles_per_seq == 0)
    def _():
        tail_ref[...] = jnp.zeros_like(tail_ref)

    d = o_ref.shape[1]
    proj = _bdot(x_ref[...].astype(BF16), w_ref[...])
    gate_b = proj[:, :d]
    u = proj[:, d:2 * d] * proj[:, 2 * d:]
    tm = u.shape[0]
    row = lax.broadcasted_iota(I32, u.shape, 0)
    prev1 = jnp.where(row == 0, tail_ref[1:2, :], pltpu.roll(u, 1, 0))
    prev2 = jnp.where(row == 0, tail_ref[0:1, :],
                      jnp.where(row == 1, tail_ref[1:2, :], pltpu.roll(u, 2, 0)))
    conv = cw_ref[0:1, :] * prev2 + cw_ref[1:2, :] * prev1 + cw_ref[2:3, :] * u
    o_ref[...] = (gate_b * conv).astype(o_ref.dtype)
    tail_ref[...] = u[tm - 2:tm, :]


def _conv_mix(x2d, w_in, conv_w, seq):
    t, d = x2d.shape
    return pl.pallas_call(
        functools.partial(_conv_kernel, tiles_per_seq=seq // ROW_TILE),
        grid=(t // ROW_TILE,),
        in_specs=[pl.BlockSpec((ROW_TILE, d), lambda i: (i, 0)),
                  pl.BlockSpec((d, 3 * d), lambda i: (0, 0)),
                  pl.BlockSpec(conv_w.shape, lambda i: (0, 0))],
        out_specs=pl.BlockSpec((ROW_TILE, d), lambda i: (i, 0)),
        out_shape=jax.ShapeDtypeStruct((t, d), BF16),
        scratch_shapes=[pltpu.VMEM((2, d), F32)],
        compiler_params=_params("arbitrary"),
    )(x2d, w_in, conv_w)


def _head_views(q):
    lane = lax.broadcasted_iota(I32, q.shape, 1)
    zero = jnp.zeros_like(q)
    return [jnp.where((lane // HEAD_DIM) == h, q, zero) for h in range(LANES // HEAD_DIM)]


def _merge_heads(accs):
    lane = lax.broadcasted_iota(I32, accs[0].shape, 1)
    out = accs[-1]
    for h in range(len(accs) - 2, -1, -1):
        out = jnp.where((lane // HEAD_DIM) == h, accs[h], out)
    return out


def _sb_kernel(q_ref, k_ref, v_ref, o_ref):
    tq = q_ref.shape[1]
    tk = ATT_K
    qi = pl.program_id(2)
    kd = (qi * tq) // tk
    qs = _head_views(q_ref[0] * jnp.asarray(HEAD_DIM ** -0.5, BF16))
    row = lax.broadcasted_iota(I32, (tq, tk), 0)
    col = lax.broadcasted_iota(I32, (tq, tk), 1)
    kr = lax.broadcasted_iota(I32, (tk, tk), 0)
    kc = lax.broadcasted_iota(I32, (tk, tk), 1)
    neg_later = jnp.where(kr > kc, -1.0, 0.0).astype(BF16)
    strict = col < row + (qi * tq - kd * tk)

    def blocks(kbs, carry, diagonal):
        ks, vs, zs, sps, spms, boths = [], [], [], [], [], []
        for kb in kbs:
            start = pl.multiple_of(kb * tk, tk)
            ks.append(k_ref[0, pl.ds(start, tk), :])
            vs.append(v_ref[0, pl.ds(start, tk), :])
        for k in ks:
            zs.append([lax.dot_general(qh, k, _NT, preferred_element_type=F32) for qh in qs])
        for zb in zs:
            spb, spmb, bothb = [], [], []
            for z in zb:
                sp = jnp.maximum(z, 0.0) + jnp.log(1.0 + jnp.exp(-jnp.abs(z)))
                spm = jnp.where(strict, sp, 0.0) if diagonal else sp
                hi = spm.astype(BF16)
                lo = (spm - hi.astype(F32)).astype(BF16)
                spb.append(sp)
                spmb.append(spm)
                bothb.append(_bdot(jnp.concatenate([hi, lo], axis=0), neg_later))
            sps.append(spb)
            spms.append(spmb)
            boths.append(bothb)
        accs = [acc for acc, _ in carry]
        runs = [run for _, run in carry]
        ws = []
        for zb, spb, spmb, bothb in zip(zs, sps, spms, boths):
            wb = []
            for h in range(len(qs)):
                inner = bothb[h][:tq] + bothb[h][tq:]
                w = jnp.exp((zb[h] - spb[h]) + (inner + runs[h]))
                if diagonal:
                    w = jnp.where(strict, w, 0.0)
                wb.append(w.astype(BF16))
                runs[h] = runs[h] + (inner[:, 0:1] - spmb[h][:, 0:1])
            ws.append(wb)
        for wb, v in zip(ws, vs):
            for h in range(len(qs)):
                accs[h] = accs[h] + _bdot(wb[h], v)
        return tuple(zip(accs, runs))

    init = tuple((jnp.zeros((tq, LANES), F32), jnp.zeros((tq, 1), F32)) for _ in qs)
    carry = blocks([kd], init, True)
    u = ATT_UNROLL
    carry = lax.fori_loop(0, kd // u, lambda j, c: blocks([kd - 1 - u * j - i for i in range(u)], c, False), carry)
    rem = kd % u
    carry = lax.fori_loop(0, rem // 2, lambda j, c: blocks([rem - 1, rem - 2], c, False), carry)
    carry = lax.fori_loop(0, rem % 2, lambda j, c: blocks([0], c, False), carry)
    o_ref[0] = _merge_heads([acc for acc, _ in carry]).astype(o_ref.dtype)


def _fox_kernel(q_ref, k_ref, v_ref, cq_ref, ck_ref, o_ref):
    tq = q_ref.shape[1]
    tk = ATT_K
    qi = pl.program_id(2)
    hp = pl.program_id(1)
    kd = (qi * tq) // tk
    n_h = LANES // HEAD_DIM
    qs = _head_views(q_ref[0] * jnp.asarray(HEAD_DIM ** -0.5, BF16))
    row = lax.broadcasted_iota(I32, (tq, tk), 0)
    col = lax.broadcasted_iota(I32, (tq, tk), 1)
    causal = col <= row + (qi * tq - kd * tk)
    cq_all = cq_ref[0]
    head_lane = lax.broadcasted_iota(I32, cq_all.shape, 1)
    cqs = [jnp.sum(jnp.where(head_lane == hp * n_h + h, cq_all, 0.0), axis=1, keepdims=True)
           for h in range(n_h)]

    def blocks(kbs, carry, diagonal):
        zs, vs = [], []
        for kb in kbs:
            start = pl.multiple_of(kb * tk, tk)
            k = k_ref[0, pl.ds(start, tk), :]
            vs.append(v_ref[0, pl.ds(start, tk), :])
            zb = []
            for h, (qh, cq) in enumerate(zip(qs, cqs)):
                ck = ck_ref[0, pl.ds(hp * n_h + h, 1), pl.ds(start, tk)]
                z = lax.dot_general(qh, k, _NT, preferred_element_type=F32) + (cq - ck)
                zb.append(jnp.where(causal, z, -jnp.inf) if diagonal else z)
            zs.append(zb)
        accs = [acc for acc, _, _ in carry]
        ms = [m for _, m, _ in carry]
        ls = [l for _, _, l in carry]
        ps, alphas = [], []
        for zb in zs:
            pb, ab = [], []
            for h, z in enumerate(zb):
                m_new = jnp.maximum(ms[h], jnp.max(z, axis=1, keepdims=True))
                alpha = jnp.exp(ms[h] - m_new)
                p = jnp.exp(z - m_new)
                ls[h] = ls[h] * alpha + jnp.sum(p, axis=1, keepdims=True)
                ms[h] = m_new
                pb.append(p.astype(BF16))
                ab.append(alpha)
            ps.append(pb)
            alphas.append(ab)
        for pb, ab, v in zip(ps, alphas, vs):
            for h in range(n_h):
                accs[h] = accs[h] * ab[h] + _bdot(pb[h], v)
        return tuple(zip(accs, ms, ls))

    init = tuple((jnp.zeros((tq, LANES), F32), jnp.full((tq, 1), -jnp.inf, F32), jnp.zeros((tq, 1), F32))
                 for _ in qs)
    carry = blocks([kd], init, True)
    u = ATT_UNROLL
    carry = lax.fori_loop(0, kd // u, lambda j, c: blocks([u * j + i for i in range(u)], c, False), carry)
    rem = kd % u
    carry = lax.fori_loop(0, rem // 2, lambda j, c: blocks([kd - rem, kd - rem + 1], c, False), carry)
    carry = lax.fori_loop(0, rem % 2, lambda j, c: blocks([kd - 1], c, False), carry)
    o_ref[0] = _merge_heads([acc / l for acc, _, l in carry]).astype(o_ref.dtype)


def _attention(kernel_fn, qkv, extra, extra_specs):
    b, s, d3 = qkv.shape
    d = d3 // 3
    pairs = d // LANES
    qspec = pl.BlockSpec((1, ATT_Q, LANES), lambda bi, hi, qi: (bi, qi, hi))
    kspec = pl.BlockSpec((1, s, LANES), lambda bi, hi, qi: (bi, 0, pairs + hi))
    vspec = pl.BlockSpec((1, s, LANES), lambda bi, hi, qi: (bi, 0, 2 * pairs + hi))
    return pl.pallas_call(
        kernel_fn,
        grid=(b, pairs, s // ATT_Q),
        in_specs=[qspec, kspec, vspec] + extra_specs,
        out_specs=pl.BlockSpec((1, ATT_Q, LANES), lambda bi, hi, qi: (bi, qi, hi)),
        out_shape=jax.ShapeDtypeStruct((b, s, d), BF16),
        compiler_params=_params("parallel", "parallel", "arbitrary"),
    )(qkv, qkv, qkv, *extra)


def _out_ln_kernel(o_ref, x_ref, w_ref, g_ref, b_ref, y_ref, *, alpha):
    h = alpha * x_ref[...] + _bdot(o_ref[...], w_ref[...])
    y_ref[...] = _layer_norm(h, g_ref[...], b_ref[...])


def _out_ln(o2d, x2d, w_out, g, b, alpha):
    t, d = x2d.shape
    row = lambda i: (i, 0)
    fixed = lambda i: (0, 0)
    return pl.pallas_call(
        functools.partial(_out_ln_kernel, alpha=alpha),
        grid=(t // ROW_TILE,),
        in_specs=[pl.BlockSpec((ROW_TILE, d), row), pl.BlockSpec((ROW_TILE, d), row),
                  pl.BlockSpec((d, d), fixed), pl.BlockSpec((1, d), fixed), pl.BlockSpec((1, d), fixed)],
        out_specs=pl.BlockSpec((ROW_TILE, d), row),
        out_shape=jax.ShapeDtypeStruct((t, d), F32),
        compiler_params=_params("parallel"),
    )(o2d, x2d, w_out, g, b)


def _router_kernel(x_ref, wh_ref, wl_ref, b_ref, e_ref, g_ref, r_ref, cnt_ref, carry_ref):
    i = pl.program_id(0)

    @pl.when(i == 0)
    def _():
        carry_ref[...] = jnp.zeros_like(carry_ref)

    x = x_ref[...]
    xh = x.astype(BF16)
    xl = (x - xh.astype(F32)).astype(BF16)
    wh = wh_ref[...]
    nt = functools.partial(lax.dot_general, dimension_numbers=_NT, preferred_element_type=F32)
    logits = nt(wh, xh) + nt(wl_ref[...], xh) + nt(wh, xl) + b_ref[:, 0:1]
    n_e, tm = logits.shape
    eid = lax.broadcasted_iota(I32, (n_e, tm), 0)

    work = logits
    tops, ids, sels = [], [], []
    for _ in range(TOP_K):
        m = jnp.max(work, axis=0, keepdims=True)
        idx = jnp.min(jnp.where(work == m, eid, n_e), axis=0, keepdims=True)
        sel = eid == idx
        work = jnp.where(sel, -jnp.inf, work)
        tops.append(m)
        ids.append(idx)
        sels.append(sel)
    ex = [jnp.exp(m - tops[0]) for m in tops]
    denom = ex[0] + ex[1] + ex[2] + ex[3]
    e_ref[...] = jnp.concatenate(ids, axis=0)
    g_ref[...] = jnp.concatenate([v / denom for v in ex], axis=0)

    chosen = jnp.where(sels[0] | sels[1] | sels[2] | sels[3], 1.0, 0.0)
    r = lax.broadcasted_iota(I32, (tm, tm), 0)
    c = lax.broadcasted_iota(I32, (tm, tm), 1)
    before = jnp.where(r < c, 1.0, 0.0).astype(BF16)
    seen = _bdot(chosen.astype(BF16), before) + carry_ref[:, 0:1]
    r_ref[...] = jnp.concatenate(
        [jnp.sum(jnp.where(s, seen, 0.0), axis=0, keepdims=True) for s in sels], axis=0).astype(I32)
    total = carry_ref[...] + jnp.sum(chosen, axis=1, keepdims=True)
    carry_ref[...] = total
    cnt_ref[...] = total


def _route(x2d, w_hi, w_lo, bias):
    t, d = x2d.shape
    n_e = w_hi.shape[0]
    tok = lambda i: (0, i)
    fixed = lambda i: (0, 0)
    return pl.pallas_call(
        _router_kernel,
        grid=(t // ROUTE_TILE,),
        in_specs=[pl.BlockSpec((ROUTE_TILE, d), lambda i: (i, 0)),
                  pl.BlockSpec((n_e, d), fixed), pl.BlockSpec((n_e, d), fixed),
                  pl.BlockSpec((n_e, LANES), fixed)],
        out_specs=[pl.BlockSpec((TOP_K, ROUTE_TILE), tok), pl.BlockSpec((TOP_K, ROUTE_TILE), tok),
                   pl.BlockSpec((TOP_K, ROUTE_TILE), tok), pl.BlockSpec((n_e, LANES), fixed)],
        out_shape=[jax.ShapeDtypeStruct((TOP_K, t), I32), jax.ShapeDtypeStruct((TOP_K, t), F32),
                   jax.ShapeDtypeStruct((TOP_K, t), I32), jax.ShapeDtypeStruct((n_e, LANES), F32)],
        scratch_shapes=[pltpu.VMEM((n_e, LANES), F32)],
        compiler_params=_params("arbitrary"),
    )(x2d, w_hi, w_lo, bias)


def _dest_kernel(pstart_ref, e_ref, r_ref, d_ref):
    e = e_ref[...]
    base = jnp.zeros(e.shape, I32)
    for j in range(N_EXPERTS):
        base = jnp.where(e == j, pstart_ref[j], base)
    d_ref[...] = base + r_ref[...]


def _destinations(pstart, top_e, rank):
    k, t = top_e.shape
    blk = min(t, 4096)
    spec = pl.BlockSpec((k, blk), lambda i: (0, i))
    return pl.pallas_call(
        _dest_kernel,
        grid=(t // blk,),
        in_specs=[pl.BlockSpec(memory_space=pltpu.SMEM), spec, spec],
        out_specs=spec,
        out_shape=jax.ShapeDtypeStruct((k, t), I32),
        compiler_params=_params("parallel"),
    )(pstart, top_e, rank)


def _scatter_kernel(dest_ref, x_ref, init_hbm, xs_hbm, sem):
    del init_hbm
    tile = x_ref.shape[0]

    def issue(t, _):
        for k in range(TOP_K):
            pltpu.make_async_copy(x_ref.at[pl.ds(t, 1)],
                                  xs_hbm.at[pl.ds(dest_ref[k, t], 1)], sem).start(priority=k % 2)
        return 0

    lax.fori_loop(0, tile, issue, 0, unroll=ISSUE_UNROLL)
    rows = xs_hbm.at[pl.ds(0, TOP_K * tile)]
    pltpu.make_async_copy(rows, rows, sem).wait()


def _dispatch(x2d, dest, n_rows):
    t, d = x2d.shape
    tile = min(MOVE_TILE, t)
    return pl.pallas_call(
        _scatter_kernel,
        grid=(t // tile,),
        in_specs=[pl.BlockSpec((TOP_K, tile), lambda i: (0, i), memory_space=pltpu.SMEM),
                  pl.BlockSpec((tile, d), lambda i: (i, 0)), pl.BlockSpec(memory_space=pl.ANY)],
        out_specs=pl.BlockSpec(memory_space=pl.ANY),
        out_shape=jax.ShapeDtypeStruct((n_rows, d), x2d.dtype),
        scratch_shapes=[pltpu.SemaphoreType.DMA(())],
        input_output_aliases={2: 0},
        compiler_params=_params("arbitrary"),
    )(dest, x2d, jnp.zeros((n_rows, d), x2d.dtype))


def _expert_kernel(be_ref, nu_ref, xs_ref, wg_ref, bg_ref, wu_ref, bu_ref, wd_ref, bd_ref, y_ref,
                   wg_bf, wu_bf, wd_bf):
    i = pl.program_id(0)
    used = i < nu_ref[0]
    new_expert = jnp.logical_or(i == 0, be_ref[i] != be_ref[jnp.maximum(i - 1, 0)])

    @pl.when(jnp.logical_and(used, new_expert))
    def _():
        wg_bf[...] = wg_ref[...].astype(BF16)
        wu_bf[...] = wu_ref[...].astype(BF16)
        wd_bf[...] = wd_ref[...].astype(BF16)

    @pl.when(used)
    def _():
        xb = xs_ref[...].astype(BF16)
        hg = jnp.minimum(_bdot(xb, wg_bf[...]) + bg_ref[...], SWIGLU_LIMIT)
        hu = jnp.clip(_bdot(xb, wu_bf[...]) + bu_ref[...], -SWIGLU_LIMIT, SWIGLU_LIMIT)
        a = hg * jax.nn.sigmoid(SWIGLU_ALPHA * hg) * (hu + 1.0)
        y_ref[...] = _bdot(a.astype(BF16), wd_bf[...]) + bd_ref[...]

    @pl.when(jnp.logical_not(used))
    def _():
        y_ref[...] = jnp.zeros_like(y_ref)


def _experts(layer, block_e, n_used, xs, w_gate, b_gate, w_up, b_up, w_down, b_down):
    n_rows, d = xs.shape
    f = w_gate.shape[3]
    rows = lambda i, be, nu: (i, 0)
    wsel = lambda i, be, nu: (layer, be[i], 0, 0)
    grid_spec = pltpu.PrefetchScalarGridSpec(
        num_scalar_prefetch=2,
        grid=(n_rows // EXPERT_TILE,),
        in_specs=[pl.BlockSpec((EXPERT_TILE, d), rows),
                  pl.BlockSpec((None, None, d, f), wsel), pl.BlockSpec((None, None, 1, f), wsel),
                  pl.BlockSpec((None, None, d, f), wsel), pl.BlockSpec((None, None, 1, f), wsel),
                  pl.BlockSpec((None, None, f, d), wsel), pl.BlockSpec((None, None, 1, d), wsel)],
        out_specs=pl.BlockSpec((EXPERT_TILE, d), rows),
        scratch_shapes=[pltpu.VMEM((d, f), BF16), pltpu.VMEM((d, f), BF16), pltpu.VMEM((f, d), BF16)],
    )
    return pl.pallas_call(
        _expert_kernel,
        grid_spec=grid_spec,
        out_shape=jax.ShapeDtypeStruct((n_rows, d), F32),
        compiler_params=pltpu.CompilerParams(dimension_semantics=("arbitrary",),
                                             vmem_limit_bytes=EXPERT_VMEM_LIMIT),
    )(block_e, n_used, xs, w_gate, b_gate, w_up, b_up, w_down, b_down)


def _combine_kernel(dest_ref, y_hbm, x_ref, g_ref, p_ref, wp_ref, wg_ref, bg_ref, lg_ref, lb_ref,
                    o_ref, buf, sem, *, alpha):
    tm = x_ref.shape[0]

    def issue(t, _):
        for k in range(TOP_K):
            pltpu.make_async_copy(y_hbm.at[pl.ds(dest_ref[k, t], 1)],
                                  buf.at[k, pl.ds(t, 1)], sem).start(priority=k % 2)
        return 0

    lax.fori_loop(0, tm, issue, 0, unroll=ISSUE_UNROLL)

    gates = jnp.concatenate([g_ref[...], jnp.zeros((LANES - TOP_K, tm), F32)], axis=0).T
    embed = _bdot(p_ref[...].astype(BF16), wp_ref[...])

    pltpu.make_async_copy(buf, buf, sem).wait()
    ffn = buf[0] * gates[:, 0:1]
    for k in range(1, TOP_K):
        ffn = ffn + buf[k] * gates[:, k:k + 1]
    x2 = _layer_norm(alpha * x_ref[...] + ffn, lg_ref[...], lb_ref[...])
    gate = jax.nn.sigmoid(_bdot(x2.astype(BF16), wg_ref[...]) + bg_ref[...])
    o_ref[...] = x2 + embed * gate


def _combine(dest, y, x2d, gates, p2d, w_proj, w_gate, b_gate, ln_g, ln_b, alpha):
    t, d = x2d.shape
    pd = p2d.shape[1]
    tile = min(COMBINE_TILE, t)
    row = lambda i: (i, 0)
    tok = lambda i: (0, i)
    fixed = lambda i: (0, 0)
    return pl.pallas_call(
        functools.partial(_combine_kernel, alpha=alpha),
        grid=(t // tile,),
        in_specs=[pl.BlockSpec((TOP_K, tile), tok, memory_space=pltpu.SMEM),
                  pl.BlockSpec(memory_space=pl.ANY),
                  pl.BlockSpec((tile, d), row),
                  pl.BlockSpec((TOP_K, tile), tok),
                  pl.BlockSpec((tile, pd), row),
                  pl.BlockSpec((pd, d), fixed), pl.BlockSpec((d, d), fixed),
                  pl.BlockSpec((1, d), fixed), pl.BlockSpec((1, d), fixed), pl.BlockSpec((1, d), fixed)],
        out_specs=pl.BlockSpec((tile, d), row),
        out_shape=jax.ShapeDtypeStruct((t, d), F32),
        scratch_shapes=[pltpu.VMEM((TOP_K, tile, d), F32), pltpu.SemaphoreType.DMA(())],
        compiler_params=_params("arbitrary"),
    )(dest, y, x2d, gates, p2d, w_proj, w_gate, b_gate, ln_g, ln_b)


def _moe(layer, x2d, router_w, router_b, w_gate, b_gate, w_up, b_up, w_down, b_down):
    t, d = x2d.shape
    w_t = router_w.T
    w_hi = w_t.astype(BF16)
    w_lo = (w_t - w_hi.astype(F32)).astype(BF16)
    bias = jnp.broadcast_to(router_b[:, None], (N_EXPERTS, LANES))
    top_e, gates, rank, counts = _route(x2d, w_hi, w_lo, bias)

    n_blocks = -(-t * TOP_K // EXPERT_TILE) + N_EXPERTS
    cnt = counts[:, 0].astype(I32)
    padded = (cnt + EXPERT_TILE - 1) // EXPERT_TILE * EXPERT_TILE
    pend = jnp.cumsum(padded)
    pstart = pend - padded
    block_start = jnp.arange(n_blocks, dtype=I32) * EXPERT_TILE
    block_e = jnp.minimum(jnp.sum(pend[None, :] <= block_start[:, None], axis=1), N_EXPERTS - 1).astype(I32)
    n_used = (pend[-1:] // EXPERT_TILE).astype(I32)

    dest = _destinations(pstart.astype(I32), top_e, rank)
    xs = _dispatch(x2d, dest, n_blocks * EXPERT_TILE)
    y = _experts(layer, block_e, n_used, xs, w_gate, b_gate[:, :, None, :], w_up, b_up[:, :, None, :],
                 w_down, b_down[:, :, None, :])
    return dest, y, gates


def kernel(x, p, conv_w_in, conv_w, conv_w_out, sb_w_in, sb_w_out, fox_w_in, fox_b_f, fox_w_out, ln1_g, ln1_b, ln2_g, ln2_b, router_w, router_b, exp_w_gate, exp_b_gate, exp_w_up, exp_b_up, exp_w_down, exp_b_down, ple_w_proj, ple_w_gate, ple_b_gate):
    bsz, s, d = x.shape
    depth = p.shape[0]
    t = bsz * s
    alpha = float((2 * depth) ** 0.25)
    assert d == N_HEADS * HEAD_DIM and s % ROW_TILE == 0 and s % ATT_K == 0 and ATT_K % ATT_Q == 0
    assert t % ROUTE_TILE == 0 and t % min(MOVE_TILE, t) == 0 and t % min(COMBINE_TILE, t) == 0

    xf = x.reshape(t, d)
    for i in range(depth):
        kind, j = i % N_MIXERS, i // N_MIXERS
        if kind == 0:
            mixed = _conv_mix(xf, conv_w_in[j].astype(BF16), conv_w[j], s)
            w_out = conv_w_out[j]
        elif kind == 1:
            qkv = _project(xf, sb_w_in[j].astype(BF16), BF16).reshape(bsz, s, 3 * d)
            mixed = _attention(_sb_kernel, qkv, [], []).reshape(t, d)
            w_out = sb_w_out[j]
        else:
            w_in = fox_w_in[j]
            w_f = jnp.pad(w_in[:, 3 * d:], ((0, 0), (0, LANES - N_HEADS))).astype(BF16)
            b_f = jnp.pad(fox_b_f[j], (0, LANES - N_HEADS))[None, :]
            qkv, cum = _fox_project(xf, w_in[:, :3 * d].astype(BF16), w_f, b_f, s)
            cum = cum[:, :N_HEADS].reshape(bsz, s, N_HEADS)
            cum_t = jnp.swapaxes(cum, 1, 2)
            mixed = _attention(
                _fox_kernel, qkv.reshape(bsz, s, 3 * d), [cum, cum_t],
                [pl.BlockSpec((1, ATT_Q, N_HEADS), lambda bi, hi, qi: (bi, qi, 0)),
                 pl.BlockSpec((1, N_HEADS, s), lambda bi, hi, qi: (bi, 0, 0))]).reshape(t, d)
            w_out = fox_w_out[j]
        x1 = _out_ln(mixed, xf, w_out.astype(BF16), ln1_g[i][None, :], ln1_b[i][None, :], alpha)
        dest, y, gates = _moe(i, x1, router_w[i], router_b[i], exp_w_gate, exp_b_gate, exp_w_up,
                              exp_b_up, exp_w_down, exp_b_down)
        xf = _combine(dest, y, x1, gates, p[i].reshape(t, -1), ple_w_proj[i].astype(BF16),
                      ple_w_gate[i].astype(BF16), ple_b_gate[i][None, :], ln2_g[i][None, :],
                      ln2_b[i][None, :], alpha)
    return xf.reshape(bsz, s, d)
```

```python
import functools

import jax
import jax.numpy as jnp
from jax import lax
from jax.experimental import pallas as pl
from jax.experimental.pallas import tpu as pltpu

F32 = jnp.float32
BF16 = jnp.bfloat16
I32 = jnp.int32

N_HEADS = 16
HEAD_DIM = 64
N_MIXERS = 3
N_EXPERTS = 32
TOP_K = 4
SWIGLU_LIMIT = 7.0
SWIGLU_ALPHA = 1.702
LN_EPS = 1e-5
LANES = 128
VMEM_LIMIT = 48 * 1024 * 1024
EXPERT_VMEM_LIMIT = 58 * 1024 * 1024

ROW_TILE = 512
ATT_Q = 256
ATT_K = 256
ATT_UNROLL = 4
SB_DEAD_LOG = -104.0
ROUTE_TILE = 512
EXPERT_TILE = 512
MOVE_TILE = 1024
COMBINE_TILE = 256
ISSUE_UNROLL = 4

_NT = (((1,), (1,)), ((), ()))


def _params(*sem):
    return pltpu.CompilerParams(dimension_semantics=sem, vmem_limit_bytes=VMEM_LIMIT)


def _bdot(a, b):
    return jnp.dot(a, b, preferred_element_type=F32)


def _split3(v):
    hi = v.astype(BF16)
    r = v - hi.astype(F32)
    mid = r.astype(BF16)
    lo = (r - mid.astype(F32)).astype(BF16)
    return hi, mid, lo


def _layer_norm(h, g, b):
    mu = jnp.mean(h, axis=-1, keepdims=True)
    c = h - mu
    var = jnp.mean(c * c, axis=-1, keepdims=True)
    return c * lax.rsqrt(var + LN_EPS) * g + b


def _proj_kernel(x_ref, w_ref, o_ref):
    o_ref[...] = _bdot(x_ref[...].astype(BF16), w_ref[...]).astype(o_ref.dtype)


def _project(x2d, w_bf16, out_dtype):
    t, d = x2d.shape
    n = w_bf16.shape[1]
    return pl.pallas_call(
        _proj_kernel,
        grid=(t // ROW_TILE,),
        in_specs=[pl.BlockSpec((ROW_TILE, d), lambda i: (i, 0)),
                  pl.BlockSpec((d, n), lambda i: (0, 0))],
        out_specs=pl.BlockSpec((ROW_TILE, n), lambda i: (i, 0)),
        out_shape=jax.ShapeDtypeStruct((t, n), out_dtype),
        compiler_params=_params("parallel"),
    )(x2d, w_bf16)


def _fox_proj_kernel(x_ref, w_ref, wf_ref, bf_ref, qkv_ref, cum_ref, carry_ref, *, tiles_per_seq):
    i = pl.program_id(0)

    @pl.when(i % tiles_per_seq == 0)
    def _():
        carry_ref[...] = jnp.zeros_like(carry_ref)

    xb = x_ref[...].astype(BF16)
    qkv_ref[...] = _bdot(xb, w_ref[...]).astype(qkv_ref.dtype)
    f = _bdot(xb, wf_ref[...]) + bf_ref[...]
    log_f = jnp.minimum(f, 0.0) - jnp.log1p(jnp.exp(-jnp.abs(f)))
    tm = f.shape[0]
    r = lax.broadcasted_iota(I32, (tm, tm), 0)
    c = lax.broadcasted_iota(I32, (tm, tm), 1)
    tri = jnp.where(c <= r, 1.0, 0.0).astype(BF16)
    hi, mid, lo = _split3(log_f)
    cum = _bdot(tri, hi) + _bdot(tri, mid) + _bdot(tri, lo) + carry_ref[...]
    cum_ref[...] = cum
    carry_ref[...] = cum[tm - 1:tm, :]


def _fox_project(x2d, w_qkv, w_f, b_f, seq):
    t, d = x2d.shape
    n = w_qkv.shape[1]
    return pl.pallas_call(
        functools.partial(_fox_proj_kernel, tiles_per_seq=seq // ROW_TILE),
        grid=(t // ROW_TILE,),
        in_specs=[pl.BlockSpec((ROW_TILE, d), lambda i: (i, 0)),
                  pl.BlockSpec((d, n), lambda i: (0, 0)),
                  pl.BlockSpec((d, LANES), lambda i: (0, 0)),
                  pl.BlockSpec((1, LANES), lambda i: (0, 0))],
        out_specs=[pl.BlockSpec((ROW_TILE, n), lambda i: (i, 0)),
                   pl.BlockSpec((ROW_TILE, LANES), lambda i: (i, 0))],
        out_shape=[jax.ShapeDtypeStruct((t, n), BF16),
                   jax.ShapeDtypeStruct((t, LANES), F32)],
        scratch_shapes=[pltpu.VMEM((1, LANES), F32)],
        compiler_params=_params("arbitrary"),
    )(x2d, w_qkv, w_f, b_f)


def _conv_kernel(x_ref, w_ref, cw_ref, o_ref, tail_ref, *, tiles_per_seq):
    i = pl.program_id(0)

    @pl.when(i % tiles_per_seq == 0)
    def _():
        tail_ref[...] = jnp.zeros_like(tail_ref)

    d = o_ref.shape[1]
    proj = _bdot(x_ref[...].astype(BF16), w_ref[...])
    gate_b = proj[:, :d]
    u = proj[:, d:2 * d] * proj[:, 2 * d:]
    tm = u.shape[0]
    row = lax.broadcasted_iota(I32, u.shape, 0)
    prev1 = jnp.where(row == 0, tail_ref[1:2, :], pltpu.roll(u, 1, 0))
    prev2 = jnp.where(row == 0, tail_ref[0:1, :],
                      jnp.where(row == 1, tail_ref[1:2, :], pltpu.roll(u, 2, 0)))
    conv = cw_ref[0:1, :] * prev2 + cw_ref[1:2, :] * prev1 + cw_ref[2:3, :] * u
    o_ref[...] = (gate_b * conv).astype(o_ref.dtype)
    tail_ref[...] = u[tm - 2:tm, :]


def _conv_mix(x2d, w_in, conv_w, seq):
    t, d = x2d.shape
    return pl.pallas_call(
        functools.partial(_conv_kernel, tiles_per_seq=seq // ROW_TILE),
        grid=(t // ROW_TILE,),
        in_specs=[pl.BlockSpec((ROW_TILE, d), lambda i: (i, 0)),
                  pl.BlockSpec((d, 3 * d), lambda i: (0, 0)),
                  pl.BlockSpec(conv_w.shape, lambda i: (0, 0))],
        out_specs=pl.BlockSpec((ROW_TILE, d), lambda i: (i, 0)),
        out_shape=jax.ShapeDtypeStruct((t, d), BF16),
        scratch_shapes=[pltpu.VMEM((2, d), F32)],
        compiler_params=_params("arbitrary"),
    )(x2d, w_in, conv_w)


def _head_views(q):
    lane = lax.broadcasted_iota(I32, q.shape, 1)
    zero = jnp.zeros_like(q)
    return [jnp.where((lane // HEAD_DIM) == h, q, zero) for h in range(LANES // HEAD_DIM)]


def _merge_heads(accs):
    lane = lax.broadcasted_iota(I32, accs[0].shape, 1)
    out = accs[-1]
    for h in range(len(accs) - 2, -1, -1):
        out = jnp.where((lane // HEAD_DIM) == h, accs[h], out)
    return out


def _sb_kernel(q_ref, k_ref, v_ref, o_ref):
    tq = q_ref.shape[1]
    tk = ATT_K
    qi = pl.program_id(2)
    kd = (qi * tq) // tk
    qs = _head_views(q_ref[0] * jnp.asarray(HEAD_DIM ** -0.5, BF16))
    row = lax.broadcasted_iota(I32, (tq, tk), 0)
    col = lax.broadcasted_iota(I32, (tq, tk), 1)
    kr = lax.broadcasted_iota(I32, (tk, tk), 0)
    kc = lax.broadcasted_iota(I32, (tk, tk), 1)
    neg_later = jnp.where(kr > kc, -1.0, 0.0).astype(BF16)
    strict = col < row + (qi * tq - kd * tk)

    def blocks(kbs, carry, masked=()):
        ks, vs, zs, sps, spms, boths = [], [], [], [], [], []
        for kb in kbs:
            start = pl.multiple_of(kb * tk, tk)
            ks.append(k_ref[0, pl.ds(start, tk), :])
            vs.append(v_ref[0, pl.ds(start, tk), :])
        for k in ks:
            zs.append([lax.dot_general(qh, k, _NT, preferred_element_type=F32) for qh in qs])
        for i, zb in enumerate(zs):
            spb, spmb, bothb = [], [], []
            for z in zb:
                sp = jnp.maximum(z, 0.0) + jnp.log(1.0 + jnp.exp(-jnp.abs(z)))
                spm = jnp.where(strict, sp, 0.0) if i in masked else sp
                hi = spm.astype(BF16)
                lo = (spm - hi.astype(F32)).astype(BF16)
                spb.append(sp)
                spmb.append(spm)
                bothb.append(_bdot(jnp.concatenate([hi, lo], axis=0), neg_later))
            sps.append(spb)
            spms.append(spmb)
            boths.append(bothb)
        accs = [acc for acc, _ in carry]
        runs = [run for _, run in carry]
        ws = []
        for i, (zb, spb, spmb, bothb) in enumerate(zip(zs, sps, spms, boths)):
            wb = []
            for h in range(len(qs)):
                inner = bothb[h][:tq] + bothb[h][tq:]
                w = jnp.exp((zb[h] - spb[h]) + (inner + runs[h]))
                if i in masked:
                    w = jnp.where(strict, w, 0.0)
                wb.append(w.astype(BF16))
                runs[h] = runs[h] + (inner[:, 0:1] - spmb[h][:, 0:1])
            ws.append(wb)
        for wb, v in zip(ws, vs):
            for h in range(len(qs)):
                accs[h] = accs[h] + _bdot(wb[h], v)
        return tuple(zip(accs, runs))

    def alive(carry):
        worst = carry[0][1]
        for _, run in carry[1:]:
            worst = jnp.maximum(worst, run)
        return jnp.max(worst) > SB_DEAD_LOG

    init = tuple((jnp.zeros((tq, LANES), F32), jnp.zeros((tq, 1), F32)) for _ in qs)
    carry = lax.cond(kd > 0, lambda c: blocks([kd, kd - 1], c, masked=(0,)),
                     lambda c: blocks([kd], c, masked=(0,)), init)
    left = jnp.maximum(kd - 1, 0)
    u = ATT_UNROLL

    def group(state):
        j, c, _ = state
        c = blocks([left - 1 - u * j - i for i in range(u)], c)
        return j + 1, c, alive(c)

    _, carry, live = lax.while_loop(lambda st: jnp.logical_and(st[0] < left // u, st[2]), group,
                                    (jnp.int32(0), carry, alive(carry)))
    rem = jnp.where(live, left % u, 0)
    carry = lax.fori_loop(0, rem // 2, lambda j, c: blocks([rem - 1, rem - 2], c), carry)
    carry = lax.fori_loop(0, rem % 2, lambda j, c: blocks([0], c), carry)
    o_ref[0] = _merge_heads([acc for acc, _ in carry]).astype(o_ref.dtype)


def _fox_kernel(q_ref, k_ref, v_ref, cq_ref, ck_ref, o_ref):
    tq = q_ref.shape[1]
    tk = ATT_K
    qi = pl.program_id(2)
    hp = pl.program_id(1)
    kd = (qi * tq) // tk
    n_h = LANES // HEAD_DIM
    qs = _head_views(q_ref[0] * jnp.asarray(HEAD_DIM ** -0.5, BF16))
    row = lax.broadcasted_iota(I32, (tq, tk), 0)
    col = lax.broadcasted_iota(I32, (tq, tk), 1)
    causal = col <= row + (qi * tq - kd * tk)
    cq_all = cq_ref[0]
    head_lane = lax.broadcasted_iota(I32, cq_all.shape, 1)
    cqs = [jnp.sum(jnp.where(head_lane == hp * n_h + h, cq_all, 0.0), axis=1, keepdims=True)
           for h in range(n_h)]

    def blocks(kbs, carry, diagonal):
        zs, vs = [], []
        for kb in kbs:
            start = pl.multiple_of(kb * tk, tk)
            k = k_ref[0, pl.ds(start, tk), :]
            vs.append(v_ref[0, pl.ds(start, tk), :])
            zb = []
            for h, (qh, cq) in enumerate(zip(qs, cqs)):
                ck = ck_ref[0, pl.ds(hp * n_h + h, 1), pl.ds(start, tk)]
                z = lax.dot_general(qh, k, _NT, preferred_element_type=F32) + (cq - ck)
                zb.append(jnp.where(causal, z, -jnp.inf) if diagonal else z)
            zs.append(zb)
        accs = [acc for acc, _, _ in carry]
        ms = [m for _, m, _ in carry]
        ls = [l for _, _, l in carry]
        ps, alphas = [], []
        for zb in zs:
            pb, ab = [], []
            for h, z in enumerate(zb):
                m_new = jnp.maximum(ms[h], jnp.max(z, axis=1, keepdims=True))
                alpha = jnp.exp(ms[h] - m_new)
                p = jnp.exp(z - m_new)
                ls[h] = ls[h] * alpha + jnp.sum(p, axis=1, keepdims=True)
                ms[h] = m_new
                pb.append(p.astype(BF16))
                ab.append(alpha)
            ps.append(pb)
            alphas.append(ab)
        for pb, ab, v in zip(ps, alphas, vs):
            for h in range(n_h):
                accs[h] = accs[h] * ab[h] + _bdot(pb[h], v)
        return tuple(zip(accs, ms, ls))

    init = tuple((jnp.zeros((tq, LANES), F32), jnp.full((tq, 1), -jnp.inf, F32), jnp.zeros((tq, 1), F32))
                 for _ in qs)
    carry = blocks([kd], init, True)
    u = ATT_UNROLL
    carry = lax.fori_loop(0, kd // u, lambda j, c: blocks([u * j + i for i in range(u)], c, False), carry)
    rem = kd % u
    carry = lax.fori_loop(0, rem // 2, lambda j, c: blocks([kd - rem, kd - rem + 1], c, False), carry)
    carry = lax.fori_loop(0, rem % 2, lambda j, c: blocks([kd - 1], c, False), carry)
    o_ref[0] = _merge_heads([acc / l for acc, _, l in carry]).astype(o_ref.dtype)


def _attention(kernel_fn, qkv, extra, extra_specs):
    b, s, d3 = qkv.shape
    d = d3 // 3
    pairs = d // LANES
    qspec = pl.BlockSpec((1, ATT_Q, LANES), lambda bi, hi, qi: (bi, qi, hi))
    kspec = pl.BlockSpec((1, s, LANES), lambda bi, hi, qi: (bi, 0, pairs + hi))
    vspec = pl.BlockSpec((1, s, LANES), lambda bi, hi, qi: (bi, 0, 2 * pairs + hi))
    return pl.pallas_call(
        kernel_fn,
        grid=(b, pairs, s // ATT_Q),
        in_specs=[qspec, kspec, vspec] + extra_specs,
        out_specs=pl.BlockSpec((1, ATT_Q, LANES), lambda bi, hi, qi: (bi, qi, hi)),
        out_shape=jax.ShapeDtypeStruct((b, s, d), BF16),
        compiler_params=_params("parallel", "parallel", "arbitrary"),
    )(qkv, qkv, qkv, *extra)


def _out_ln_kernel(o_ref, x_ref, w_ref, g_ref, b_ref, y_ref, *, alpha):
    h = alpha * x_ref[...] + _bdot(o_ref[...], w_ref[...])
    y_ref[...] = _layer_norm(h, g_ref[...], b_ref[...])


def _out_ln(o2d, x2d, w_out, g, b, alpha):
    t, d = x2d.shape
    row = lambda i: (i, 0)
    fixed = lambda i: (0, 0)
    return pl.pallas_call(
        functools.partial(_out_ln_kernel, alpha=alpha),
        grid=(t // ROW_TILE,),
        in_specs=[pl.BlockSpec((ROW_TILE, d), row), pl.BlockSpec((ROW_TILE, d), row),
                  pl.BlockSpec((d, d), fixed), pl.BlockSpec((1, d), fixed), pl.BlockSpec((1, d), fixed)],
        out_specs=pl.BlockSpec((ROW_TILE, d), row),
        out_shape=jax.ShapeDtypeStruct((t, d), F32),
        compiler_params=_params("parallel"),
    )(o2d, x2d, w_out, g, b)


def _router_kernel(x_ref, wh_ref, wl_ref, b_ref, e_ref, g_ref, r_ref, cnt_ref, carry_ref):
    i = pl.program_id(0)

    @pl.when(i == 0)
    def _():
        carry_ref[...] = jnp.zeros_like(carry_ref)

    x = x_ref[...]
    xh = x.astype(BF16)
    xl = (x - xh.astype(F32)).astype(BF16)
    wh = wh_ref[...]
    nt = functools.partial(lax.dot_general, dimension_numbers=_NT, preferred_element_type=F32)
    logits = nt(wh, xh) + nt(wl_ref[...], xh) + nt(wh, xl) + b_ref[:, 0:1]
    n_e, tm = logits.shape
    eid = lax.broadcasted_iota(I32, (n_e, tm), 0)

    work = logits
    tops, ids, sels = [], [], []
    for _ in range(TOP_K):
        m = jnp.max(work, axis=0, keepdims=True)
        idx = jnp.min(jnp.where(work == m, eid, n_e), axis=0, keepdims=True)
        sel = eid == idx
        work = jnp.where(sel, -jnp.inf, work)
        tops.append(m)
        ids.append(idx)
        sels.append(sel)
    ex = [jnp.exp(m - tops[0]) for m in tops]
    denom = ex[0] + ex[1] + ex[2] + ex[3]
    e_ref[...] = jnp.concatenate(ids, axis=0)
    g_ref[...] = jnp.concatenate([v / denom for v in ex], axis=0)

    chosen = jnp.where(sels[0] | sels[1] | sels[2] | sels[3], 1.0, 0.0)
    r = lax.broadcasted_iota(I32, (tm, tm), 0)
    c = lax.broadcasted_iota(I32, (tm, tm), 1)
    before = jnp.where(r < c, 1.0, 0.0).astype(BF16)
    seen = _bdot(chosen.astype(BF16), before) + carry_ref[:, 0:1]
    r_ref[...] = jnp.concatenate(
        [jnp.sum(jnp.where(s, seen, 0.0), axis=0, keepdims=True) for s in sels], axis=0).astype(I32)
    total = carry_ref[...] + jnp.sum(chosen, axis=1, keepdims=True)
    carry_ref[...] = total
    cnt_ref[...] = total


def _route(x2d, w_hi, w_lo, bias):
    t, d = x2d.shape
    n_e = w_hi.shape[0]
    tok = lambda i: (0, i)
    fixed = lambda i: (0, 0)
    return pl.pallas_call(
        _router_kernel,
        grid=(t // ROUTE_TILE,),
        in_specs=[pl.BlockSpec((ROUTE_TILE, d), lambda i: (i, 0)),
                  pl.BlockSpec((n_e, d), fixed), pl.BlockSpec((n_e, d), fixed),
                  pl.BlockSpec((n_e, LANES), fixed)],
        out_specs=[pl.BlockSpec((TOP_K, ROUTE_TILE), tok), pl.BlockSpec((TOP_K, ROUTE_TILE), tok),
                   pl.BlockSpec((TOP_K, ROUTE_TILE), tok), pl.BlockSpec((n_e, LANES), fixed)],
        out_shape=[jax.ShapeDtypeStruct((TOP_K, t), I32), jax.ShapeDtypeStruct((TOP_K, t), F32),
                   jax.ShapeDtypeStruct((TOP_K, t), I32), jax.ShapeDtypeStruct((n_e, LANES), F32)],
        scratch_shapes=[pltpu.VMEM((n_e, LANES), F32)],
        compiler_params=_params("arbitrary"),
    )(x2d, w_hi, w_lo, bias)


def _dest_kernel(pstart_ref, e_ref, r_ref, d_ref):
    e = e_ref[...]
    base = jnp.zeros(e.shape, I32)
    for j in range(N_EXPERTS):
        base = jnp.where(e == j, pstart_ref[j], base)
    d_ref[...] = base + r_ref[...]


def _destinations(pstart, top_e, rank):
    k, t = top_e.shape
    blk = min(t, 4096)
    spec = pl.BlockSpec((k, blk), lambda i: (0, i))
    return pl.pallas_call(
        _dest_kernel,
        grid=(t // blk,),
        in_specs=[pl.BlockSpec(memory_space=pltpu.SMEM), spec, spec],
        out_specs=spec,
        out_shape=jax.ShapeDtypeStruct((k, t), I32),
        compiler_params=_params("parallel"),
    )(pstart, top_e, rank)


def _scatter_kernel(dest_ref, tail_ref, x_ref, xs_hbm, zero_buf, sem, zsem):
    tile = x_ref.shape[0]

    @pl.when(pl.program_id(0) == 0)
    def _():
        zero_buf[...] = jnp.zeros_like(zero_buf)

        def zero_block(j):
            return pltpu.make_async_copy(
                zero_buf, xs_hbm.at[pl.ds(pl.multiple_of(j * EXPERT_TILE, EXPERT_TILE), EXPERT_TILE)], zsem)

        n_blocks = xs_hbm.shape[0] // EXPERT_TILE
        first_unused = tail_ref[N_EXPERTS]
        for step in ("start", "wait"):
            for e in range(N_EXPERTS):
                @pl.when(tail_ref[e] >= 0)
                def _(e=e, step=step):
                    getattr(zero_block(tail_ref[e]), step)()

            @pl.loop(first_unused, n_blocks)
            def _(j, step=step):
                getattr(zero_block(j), step)()

    def issue(t, _):
        for k in range(TOP_K):
            pltpu.make_async_copy(x_ref.at[pl.ds(t, 1)],
                                  xs_hbm.at[pl.ds(dest_ref[k, t], 1)], sem).start(priority=k % 2)
        return 0

    lax.fori_loop(0, tile, issue, 0, unroll=ISSUE_UNROLL)
    rows = xs_hbm.at[pl.ds(0, TOP_K * tile)]
    pltpu.make_async_copy(rows, rows, sem).wait()


def _dispatch(x2d, dest, tails, n_rows):
    t, d = x2d.shape
    tile = min(MOVE_TILE, t)
    return pl.pallas_call(
        _scatter_kernel,
        grid=(t // tile,),
        in_specs=[pl.BlockSpec((TOP_K, tile), lambda i: (0, i), memory_space=pltpu.SMEM),
                  pl.BlockSpec(memory_space=pltpu.SMEM),
                  pl.BlockSpec((tile, d), lambda i: (i, 0))],
        out_specs=pl.BlockSpec(memory_space=pl.ANY),
        out_shape=jax.ShapeDtypeStruct((n_rows, d), x2d.dtype),
        scratch_shapes=[pltpu.VMEM((EXPERT_TILE, d), x2d.dtype), pltpu.SemaphoreType.DMA(()),
                        pltpu.SemaphoreType.DMA(())],
        compiler_params=_params("arbitrary"),
    )(dest, tails, x2d)


def _expert_kernel(be_ref, nu_ref, xs_ref, wg_ref, bg_ref, wu_ref, bu_ref, wd_ref, bd_ref, y_ref,
                   wg_bf, wu_bf, wd_bf):
    i = pl.program_id(0)
    used = i < nu_ref[0]
    new_expert = jnp.logical_or(i == 0, be_ref[i] != be_ref[jnp.maximum(i - 1, 0)])

    @pl.when(jnp.logical_and(used, new_expert))
    def _():
        wg_bf[...] = wg_ref[...].astype(BF16)
        wu_bf[...] = wu_ref[...].astype(BF16)
        wd_bf[...] = wd_ref[...].astype(BF16)

    @pl.when(used)
    def _():
        xb = xs_ref[...].astype(BF16)
        hg = jnp.minimum(_bdot(xb, wg_bf[...]) + bg_ref[...], SWIGLU_LIMIT)
        hu = jnp.clip(_bdot(xb, wu_bf[...]) + bu_ref[...], -SWIGLU_LIMIT, SWIGLU_LIMIT)
        a = hg * jax.nn.sigmoid(SWIGLU_ALPHA * hg) * (hu + 1.0)
        y_ref[...] = _bdot(a.astype(BF16), wd_bf[...]) + bd_ref[...]

    @pl.when(jnp.logical_not(used))
    def _():
        y_ref[...] = jnp.zeros_like(y_ref)


def _experts(layer, block_e, n_used, xs, w_gate, b_gate, w_up, b_up, w_down, b_down):
    n_rows, d = xs.shape
    f = w_gate.shape[3]
    rows = lambda i, be, nu: (i, 0)
    used_rows = lambda i, be, nu: (jnp.minimum(i, jnp.maximum(nu[0] - 1, 0)), 0)
    wsel = lambda i, be, nu: (layer, be[i], 0, 0)
    grid_spec = pltpu.PrefetchScalarGridSpec(
        num_scalar_prefetch=2,
        grid=(n_rows // EXPERT_TILE,),
        in_specs=[pl.BlockSpec((EXPERT_TILE, d), used_rows),
                  pl.BlockSpec((None, None, d, f), wsel), pl.BlockSpec((None, None, 1, f), wsel),
                  pl.BlockSpec((None, None, d, f), wsel), pl.BlockSpec((None, None, 1, f), wsel),
                  pl.BlockSpec((None, None, f, d), wsel), pl.BlockSpec((None, None, 1, d), wsel)],
        out_specs=pl.BlockSpec((EXPERT_TILE, d), rows),
        scratch_shapes=[pltpu.VMEM((d, f), BF16), pltpu.VMEM((d, f), BF16), pltpu.VMEM((f, d), BF16)],
    )
    return pl.pallas_call(
        _expert_kernel,
        grid_spec=grid_spec,
        out_shape=jax.ShapeDtypeStruct((n_rows, d), F32),
        compiler_params=pltpu.CompilerParams(dimension_semantics=("arbitrary",),
                                             vmem_limit_bytes=EXPERT_VMEM_LIMIT),
    )(block_e, n_used, xs, w_gate, b_gate, w_up, b_up, w_down, b_down)


def _combine_kernel(dest_ref, y_hbm, x_ref, g_ref, p_ref, wp_ref, wg_ref, bg_ref, lg_ref, lb_ref,
                    o_ref, buf, sem, *, alpha):
    tm = x_ref.shape[0]

    def issue(t, _):
        for k in range(TOP_K):
            pltpu.make_async_copy(y_hbm.at[pl.ds(dest_ref[k, t], 1)],
                                  buf.at[k, pl.ds(t, 1)], sem).start(priority=k % 2)
        return 0

    lax.fori_loop(0, tm, issue, 0, unroll=ISSUE_UNROLL)

    gates = jnp.concatenate([g_ref[...], jnp.zeros((LANES - TOP_K, tm), F32)], axis=0).T
    embed = _bdot(p_ref[...].astype(BF16), wp_ref[...])

    pltpu.make_async_copy(buf, buf, sem).wait()
    ffn = buf[0] * gates[:, 0:1]
    for k in range(1, TOP_K):
        ffn = ffn + buf[k] * gates[:, k:k + 1]
    x2 = _layer_norm(alpha * x_ref[...] + ffn, lg_ref[...], lb_ref[...])
    gate = jax.nn.sigmoid(_bdot(x2.astype(BF16), wg_ref[...]) + bg_ref[...])
    o_ref[...] = x2 + embed * gate


def _combine(dest, y, x2d, gates, p2d, w_proj, w_gate, b_gate, ln_g, ln_b, alpha):
    t, d = x2d.shape
    pd = p2d.shape[1]
    tile = min(COMBINE_TILE, t)
    row = lambda i: (i, 0)
    tok = lambda i: (0, i)
    fixed = lambda i: (0, 0)
    return pl.pallas_call(
        functools.partial(_combine_kernel, alpha=alpha),
        grid=(t // tile,),
        in_specs=[pl.BlockSpec((TOP_K, tile), tok, memory_space=pltpu.SMEM),
                  pl.BlockSpec(memory_space=pl.ANY),
                  pl.BlockSpec((tile, d), row),
                  pl.BlockSpec((TOP_K, tile), tok),
                  pl.BlockSpec((tile, pd), row),
                  pl.BlockSpec((pd, d), fixed), pl.BlockSpec((d, d), fixed),
                  pl.BlockSpec((1, d), fixed), pl.BlockSpec((1, d), fixed), pl.BlockSpec((1, d), fixed)],
        out_specs=pl.BlockSpec((tile, d), row),
        out_shape=jax.ShapeDtypeStruct((t, d), F32),
        scratch_shapes=[pltpu.VMEM((TOP_K, tile, d), F32), pltpu.SemaphoreType.DMA(())],
        compiler_params=_params("arbitrary"),
    )(dest, y, x2d, gates, p2d, w_proj, w_gate, b_gate, ln_g, ln_b)


def _moe(layer, x2d, router_w, router_b, w_gate, b_gate, w_up, b_up, w_down, b_down):
    t, d = x2d.shape
    w_t = router_w.T
    w_hi = w_t.astype(BF16)
    w_lo = (w_t - w_hi.astype(F32)).astype(BF16)
    bias = jnp.broadcast_to(router_b[:, None], (N_EXPERTS, LANES))
    top_e, gates, rank, counts = _route(x2d, w_hi, w_lo, bias)

    n_blocks = -(-t * TOP_K // EXPERT_TILE) + N_EXPERTS
    cnt = counts[:, 0].astype(I32)
    padded = (cnt + EXPERT_TILE - 1) // EXPERT_TILE * EXPERT_TILE
    pend = jnp.cumsum(padded)
    pstart = pend - padded
    block_start = jnp.arange(n_blocks, dtype=I32) * EXPERT_TILE
    block_e = jnp.minimum(jnp.sum(pend[None, :] <= block_start[:, None], axis=1), N_EXPERTS - 1).astype(I32)
    n_used = (pend[-1:] // EXPERT_TILE).astype(I32)

    dest = _destinations(pstart.astype(I32), top_e, rank)
    n_rows = n_blocks * EXPERT_TILE
    tails = jnp.where(cnt > 0, pend // EXPERT_TILE - 1, -1)
    tails = jnp.concatenate([tails, n_used]).astype(I32)
    xs = _dispatch(x2d, dest, tails, n_rows)
    y = _experts(layer, block_e, n_used, xs, w_gate, b_gate[:, :, None, :], w_up, b_up[:, :, None, :],
                 w_down, b_down[:, :, None, :])
    return dest, y, gates


def kernel(x, p, conv_w_in, conv_w, conv_w_out, sb_w_in, sb_w_out, fox_w_in, fox_b_f, fox_w_out, ln1_g, ln1_b, ln2_g, ln2_b, router_w, router_b, exp_w_gate, exp_b_gate, exp_w_up, exp_b_up, exp_w_down, exp_b_down, ple_w_proj, ple_w_gate, ple_b_gate):
    bsz, s, d = x.shape
    depth = p.shape[0]
    t = bsz * s
    alpha = float((2 * depth) ** 0.25)
    assert d == N_HEADS * HEAD_DIM and s % ROW_TILE == 0 and s % ATT_K == 0 and ATT_K % ATT_Q == 0
    assert t % ROUTE_TILE == 0 and t % min(MOVE_TILE, t) == 0 and t % min(COMBINE_TILE, t) == 0

    xf = x.reshape(t, d)
    for i in range(depth):
        kind, j = i % N_MIXERS, i // N_MIXERS
        if kind == 0:
            mixed = _conv_mix(xf, conv_w_in[j].astype(BF16), conv_w[j], s)
            w_out = conv_w_out[j]
        elif kind == 1:
            qkv = _project(xf, sb_w_in[j].astype(BF16), BF16).reshape(bsz, s, 3 * d)
            mixed = _attention(_sb_kernel, qkv, [], []).reshape(t, d)
            w_out = sb_w_out[j]
        else:
            w_in = fox_w_in[j]
            w_f = jnp.pad(w_in[:, 3 * d:], ((0, 0), (0, LANES - N_HEADS))).astype(BF16)
            b_f = jnp.pad(fox_b_f[j], (0, LANES - N_HEADS))[None, :]
            qkv, cum = _fox_project(xf, w_in[:, :3 * d].astype(BF16), w_f, b_f, s)
            cum = cum[:, :N_HEADS].reshape(bsz, s, N_HEADS)
            cum_t = jnp.swapaxes(cum, 1, 2)
            mixed = _attention(
                _fox_kernel, qkv.reshape(bsz, s, 3 * d), [cum, cum_t],
                [pl.BlockSpec((1, ATT_Q, N_HEADS), lambda bi, hi, qi: (bi, qi, 0)),
                 pl.BlockSpec((1, N_HEADS, s), lambda bi, hi, qi: (bi, 0, 0))]).reshape(t, d)
            w_out = fox_w_out[j]
        x1 = _out_ln(mixed, xf, w_out.astype(BF16), ln1_g[i][None, :], ln1_b[i][None, :], alpha)
        dest, y, gates = _moe(i, x1, router_w[i], router_b[i], exp_w_gate, exp_b_gate, exp_w_up,
                              exp_b_up, exp_w_down, exp_b_down)
        xf = _combine(dest, y, x1, gates, p[i].reshape(t, -1), ple_w_proj[i].astype(BF16),
                      ple_w_gate[i].astype(BF16), ple_b_gate[i][None, :], ln2_g[i][None, :],
                      ln2_b[i][None, :], alpha)
    return xf.reshape(bsz, s, d)
```

```python
import functools

import jax
import jax.numpy as jnp
from jax import lax
from jax.experimental import pallas as pl
from jax.experimental.pallas import tpu as pltpu

F32 = jnp.float32
BF16 = jnp.bfloat16
I32 = jnp.int32

N_HEADS = 16
HEAD_DIM = 64
N_MIXERS = 3
N_EXPERTS = 32
TOP_K = 4
SWIGLU_LIMIT = 7.0
SWIGLU_ALPHA = 1.702
LN_EPS = 1e-5
LANES = 128
VMEM_LIMIT = 48 * 1024 * 1024
EXPERT_VMEM_LIMIT = 58 * 1024 * 1024

ROW_TILE = 512
ATT_Q = 256
ATT_K = 256
ATT_UNROLL = 4
SB_DEAD_LOG = -104.0
ROUTE_TILE = 512
EXPERT_TILE = 512
MOVE_TILE = 1024
COMBINE_TILE = 256
ISSUE_UNROLL = 4

_NT = (((1,), (1,)), ((), ()))


def _params(*sem):
    return pltpu.CompilerParams(dimension_semantics=sem, vmem_limit_bytes=VMEM_LIMIT)


def _bdot(a, b):
    return jnp.dot(a, b, preferred_element_type=F32)


def _split3(v):
    hi = v.astype(BF16)
    r = v - hi.astype(F32)
    mid = r.astype(BF16)
    lo = (r - mid.astype(F32)).astype(BF16)
    return hi, mid, lo


def _layer_norm(h, g, b):
    mu = jnp.mean(h, axis=-1, keepdims=True)
    c = h - mu
    var = jnp.mean(c * c, axis=-1, keepdims=True)
    return c * lax.rsqrt(var + LN_EPS) * g + b


def _proj_kernel(x_ref, w_ref, o_ref):
    o_ref[...] = _bdot(x_ref[...].astype(BF16), w_ref[...]).astype(o_ref.dtype)


def _project(x2d, w_bf16, out_dtype):
    t, d = x2d.shape
    n = w_bf16.shape[1]
    return pl.pallas_call(
        _proj_kernel,
        grid=(t // ROW_TILE,),
        in_specs=[pl.BlockSpec((ROW_TILE, d), lambda i: (i, 0)),
                  pl.BlockSpec((d, n), lambda i: (0, 0))],
        out_specs=pl.BlockSpec((ROW_TILE, n), lambda i: (i, 0)),
        out_shape=jax.ShapeDtypeStruct((t, n), out_dtype),
        compiler_params=_params("parallel"),
    )(x2d, w_bf16)


def _fox_proj_kernel(x_ref, w_ref, wf_ref, bf_ref, qkv_ref, cum_ref, carry_ref, *, tiles_per_seq):
    i = pl.program_id(0)

    @pl.when(i % tiles_per_seq == 0)
    def _():
        carry_ref[...] = jnp.zeros_like(carry_ref)

    xb = x_ref[...].astype(BF16)
    qkv_ref[...] = _bdot(xb, w_ref[...]).astype(qkv_ref.dtype)
    f = _bdot(xb, wf_ref[...]) + bf_ref[...]
    log_f = jnp.minimum(f, 0.0) - jnp.log1p(jnp.exp(-jnp.abs(f)))
    tm = f.shape[0]
    r = lax.broadcasted_iota(I32, (tm, tm), 0)
    c = lax.broadcasted_iota(I32, (tm, tm), 1)
    tri = jnp.where(c <= r, 1.0, 0.0).astype(BF16)
    hi, mid, lo = _split3(log_f)
    cum = _bdot(tri, hi) + _bdot(tri, mid) + _bdot(tri, lo) + carry_ref[...]
    cum_ref[...] = cum
    carry_ref[...] = cum[tm - 1:tm, :]


def _fox_project(x2d, w_qkv, w_f, b_f, seq):
    t, d = x2d.shape
    n = w_qkv.shape[1]
    return pl.pallas_call(
        functools.partial(_fox_proj_kernel, tiles_per_seq=seq // ROW_TILE),
        grid=(t // ROW_TILE,),
        in_specs=[pl.BlockSpec((ROW_TILE, d), lambda i: (i, 0)),
                  pl.BlockSpec((d, n), lambda i: (0, 0)),
                  pl.BlockSpec((d, LANES), lambda i: (0, 0)),
                  pl.BlockSpec((1, LANES), lambda i: (0, 0))],
        out_specs=[pl.BlockSpec((ROW_TILE, n), lambda i: (i, 0)),
                   pl.BlockSpec((ROW_TILE, LANES), lambda i: (i, 0))],
        out_shape=[jax.ShapeDtypeStruct((t, n), BF16),
                   jax.ShapeDtypeStruct((t, LANES), F32)],
        scratch_shapes=[pltpu.VMEM((1, LANES), F32)],
        compiler_params=_params("arbitrary"),
    )(x2d, w_qkv, w_f, b_f)


def _conv_kernel(x_ref, w_ref, cw_ref, o_ref, tail_ref, *, tiles_per_seq):
    i = pl.program_id(0)

    @pl.when(i % tiles_per_seq == 0)
    def _():
        tail_ref[...] = jnp.zeros_like(tail_ref)

    d = o_ref.shape[1]
    proj = _bdot(x_ref[...].astype(BF16), w_ref[...])
    gate_b = proj[:, :d]
    u = proj[:, d:2 * d] * proj[:, 2 * d:]
    tm = u.shape[0]
    row = lax.broadcasted_iota(I32, u.shape, 0)
    prev1 = jnp.where(row == 0, tail_ref[1:2, :], pltpu.roll(u, 1, 0))
    prev2 = jnp.where(row == 0, tail_ref[0:1, :],
                      jnp.where(row == 1, tail_ref[1:2, :], pltpu.roll(u, 2, 0)))
    conv = cw_ref[0:1, :] * prev2 + cw_ref[1:2, :] * prev1 + cw_ref[2:3, :] * u
    o_ref[...] = (gate_b * conv).astype(o_ref.dtype)
    tail_ref[...] = u[tm - 2:tm, :]


def _conv_mix(x2d, w_in, conv_w, seq):
    t, d = x2d.shape
    return pl.pallas_call(
        functools.partial(_conv_kernel, tiles_per_seq=seq // ROW_TILE),
        grid=(t // ROW_TILE,),
        in_specs=[pl.BlockSpec((ROW_TILE, d), lambda i: (i, 0)),
                  pl.BlockSpec((d, 3 * d), lambda i: (0, 0)),
                  pl.BlockSpec(conv_w.shape, lambda i: (0, 0))],
        out_specs=pl.BlockSpec((ROW_TILE, d), lambda i: (i, 0)),
        out_shape=jax.ShapeDtypeStruct((t, d), BF16),
        scratch_shapes=[pltpu.VMEM((2, d), F32)],
        compiler_params=_params("arbitrary"),
    )(x2d, w_in, conv_w)


def _head_views(q):
    lane = lax.broadcasted_iota(I32, q.shape, 1)
    zero = jnp.zeros_like(q)
    return [jnp.where((lane // HEAD_DIM) == h, q, zero) for h in range(LANES // HEAD_DIM)]


def _merge_heads(accs):
    lane = lax.broadcasted_iota(I32, accs[0].shape, 1)
    out = accs[-1]
    for h in range(len(accs) - 2, -1, -1):
        out = jnp.where((lane // HEAD_DIM) == h, accs[h], out)
    return out


def _sb_kernel(q_ref, k_ref, v_ref, o_ref):
    tq = q_ref.shape[1]
    tk = ATT_K
    qi = pl.program_id(2)
    kd = (qi * tq) // tk
    qs = _head_views(q_ref[0] * jnp.asarray(HEAD_DIM ** -0.5, BF16))
    row = lax.broadcasted_iota(I32, (tq, tk), 0)
    col = lax.broadcasted_iota(I32, (tq, tk), 1)
    kr = lax.broadcasted_iota(I32, (tk, tk), 0)
    kc = lax.broadcasted_iota(I32, (tk, tk), 1)
    neg_later = jnp.where(kr > kc, -1.0, 0.0).astype(BF16)
    strict = col < row + (qi * tq - kd * tk)

    def blocks(kbs, carry, masked=()):
        ks, vs, zs, sps, spms, boths = [], [], [], [], [], []
        for kb in kbs:
            start = pl.multiple_of(kb * tk, tk)
            ks.append(k_ref[0, pl.ds(start, tk), :])
            vs.append(v_ref[0, pl.ds(start, tk), :])
        for k in ks:
            zs.append([lax.dot_general(qh, k, _NT, preferred_element_type=F32) for qh in qs])
        for i, zb in enumerate(zs):
            spb, spmb, bothb = [], [], []
            for z in zb:
                sp = jnp.maximum(z, 0.0) + jnp.log(1.0 + jnp.exp(-jnp.abs(z)))
                spm = jnp.where(strict, sp, 0.0) if i in masked else sp
                hi = spm.astype(BF16)
                lo = (spm - hi.astype(F32)).astype(BF16)
                spb.append(sp)
                spmb.append(spm)
                bothb.append(_bdot(jnp.concatenate([hi, lo], axis=0), neg_later))
            sps.append(spb)
            spms.append(spmb)
            boths.append(bothb)
        accs = [acc for acc, _ in carry]
        runs = [run for _, run in carry]
        ws = []
        for i, (zb, spb, spmb, bothb) in enumerate(zip(zs, sps, spms, boths)):
            wb = []
            for h in range(len(qs)):
                inner = bothb[h][:tq] + bothb[h][tq:]
                w = jnp.exp((zb[h] - spb[h]) + (inner + runs[h]))
                if i in masked:
                    w = jnp.where(strict, w, 0.0)
                wb.append(w.astype(BF16))
                runs[h] = runs[h] + (inner[:, 0:1] - spmb[h][:, 0:1])
            ws.append(wb)
        for wb, v in zip(ws, vs):
            for h in range(len(qs)):
                accs[h] = accs[h] + _bdot(wb[h], v)
        return tuple(zip(accs, runs))

    def alive(carry):
        worst = carry[0][1]
        for _, run in carry[1:]:
            worst = jnp.maximum(worst, run)
        return jnp.max(worst) > SB_DEAD_LOG

    init = tuple((jnp.zeros((tq, LANES), F32), jnp.zeros((tq, 1), F32)) for _ in qs)
    carry = lax.cond(kd > 0, lambda c: blocks([kd, kd - 1], c, masked=(0,)),
                     lambda c: blocks([kd], c, masked=(0,)), init)
    left = jnp.maximum(kd - 1, 0)
    u = ATT_UNROLL

    def group(state):
        j, c, _ = state
        c = blocks([left - 1 - u * j - i for i in range(u)], c)
        return j + 1, c, alive(c)

    _, carry, live = lax.while_loop(lambda st: jnp.logical_and(st[0] < left // u, st[2]), group,
                                    (jnp.int32(0), carry, alive(carry)))
    rem = jnp.where(live, left % u, 0)
    carry = lax.fori_loop(0, rem // 2, lambda j, c: blocks([rem - 1, rem - 2], c), carry)
    carry = lax.fori_loop(0, rem % 2, lambda j, c: blocks([0], c), carry)
    o_ref[0] = _merge_heads([acc for acc, _ in carry]).astype(o_ref.dtype)


def _fox_kernel(q_ref, k_ref, v_ref, cq_ref, ck_ref, o_ref):
    tq = q_ref.shape[1]
    tk = ATT_K
    qi = pl.program_id(2)
    hp = pl.program_id(1)
    kd = (qi * tq) // tk
    n_h = LANES // HEAD_DIM
    qs = _head_views(q_ref[0] * jnp.asarray(HEAD_DIM ** -0.5, BF16))
    row = lax.broadcasted_iota(I32, (tq, tk), 0)
    col = lax.broadcasted_iota(I32, (tq, tk), 1)
    causal = col <= row + (qi * tq - kd * tk)
    cq_all = cq_ref[0]
    head_lane = lax.broadcasted_iota(I32, cq_all.shape, 1)
    cqs = [jnp.sum(jnp.where(head_lane == hp * n_h + h, cq_all, 0.0), axis=1, keepdims=True)
           for h in range(n_h)]

    def blocks(kbs, carry, diagonal):
        zs, vs = [], []
        for kb in kbs:
            start = pl.multiple_of(kb * tk, tk)
            k = k_ref[0, pl.ds(start, tk), :]
            vs.append(v_ref[0, pl.ds(start, tk), :])
            zb = []
            for h, (qh, cq) in enumerate(zip(qs, cqs)):
                ck = ck_ref[0, pl.ds(hp * n_h + h, 1), pl.ds(start, tk)]
                z = lax.dot_general(qh, k, _NT, preferred_element_type=F32) + (cq - ck)
                zb.append(jnp.where(causal, z, -jnp.inf) if diagonal else z)
            zs.append(zb)
        accs = [acc for acc, _, _ in carry]
        ms = [m for _, m, _ in carry]
        ls = [l for _, _, l in carry]
        ps, alphas = [], []
        for zb in zs:
            pb, ab = [], []
            for h, z in enumerate(zb):
                m_new = jnp.maximum(ms[h], jnp.max(z, axis=1, keepdims=True))
                alpha = jnp.exp(ms[h] - m_new)
                p = jnp.exp(z - m_new)
                ls[h] = ls[h] * alpha + jnp.sum(p, axis=1, keepdims=True)
                ms[h] = m_new
                pb.append(p.astype(BF16))
                ab.append(alpha)
            ps.append(pb)
            alphas.append(ab)
        for pb, ab, v in zip(ps, alphas, vs):
            for h in range(n_h):
                accs[h] = accs[h] * ab[h] + _bdot(pb[h], v)
        return tuple(zip(accs, ms, ls))

    init = tuple((jnp.zeros((tq, LANES), F32), jnp.full((tq, 1), -jnp.inf, F32), jnp.zeros((tq, 1), F32))
                 for _ in qs)
    carry = blocks([kd], init, True)
    u = ATT_UNROLL
    carry = lax.fori_loop(0, kd // u, lambda j, c: blocks([u * j + i for i in range(u)], c, False), carry)
    rem = kd % u
    carry = lax.fori_loop(0, rem // 2, lambda j, c: blocks([kd - rem, kd - rem + 1], c, False), carry)
    carry = lax.fori_loop(0, rem % 2, lambda j, c: blocks([kd - 1], c, False), carry)
    o_ref[0] = _merge_heads([acc / l for acc, _, l in carry]).astype(o_ref.dtype)


def _attention(kernel_fn, qkv, extra, extra_specs):
    b, s, d3 = qkv.shape
    d = d3 // 3
    pairs = d // LANES
    qspec = pl.BlockSpec((1, ATT_Q, LANES), lambda bi, hi, qi: (bi, qi, hi))
    kspec = pl.BlockSpec((1, s, LANES), lambda bi, hi, qi: (bi, 0, pairs + hi))
    vspec = pl.BlockSpec((1, s, LANES), lambda bi, hi, qi: (bi, 0, 2 * pairs + hi))
    return pl.pallas_call(
        kernel_fn,
        grid=(b, pairs, s // ATT_Q),
        in_specs=[qspec, kspec, vspec] + extra_specs,
        out_specs=pl.BlockSpec((1, ATT_Q, LANES), lambda bi, hi, qi: (bi, qi, hi)),
        out_shape=jax.ShapeDtypeStruct((b, s, d), BF16),
        compiler_params=_params("parallel", "parallel", "arbitrary"),
    )(qkv, qkv, qkv, *extra)


def _out_ln_kernel(o_ref, x_ref, w_ref, g_ref, b_ref, y_ref, *, alpha):
    h = alpha * x_ref[...] + _bdot(o_ref[...], w_ref[...])
    y_ref[...] = _layer_norm(h, g_ref[...], b_ref[...])


def _out_ln(o2d, x2d, w_out, g, b, alpha):
    t, d = x2d.shape
    row = lambda i: (i, 0)
    fixed = lambda i: (0, 0)
    return pl.pallas_call(
        functools.partial(_out_ln_kernel, alpha=alpha),
        grid=(t // ROW_TILE,),
        in_specs=[pl.BlockSpec((ROW_TILE, d), row), pl.BlockSpec((ROW_TILE, d), row),
                  pl.BlockSpec((d, d), fixed), pl.BlockSpec((1, d), fixed), pl.BlockSpec((1, d), fixed)],
        out_specs=pl.BlockSpec((ROW_TILE, d), row),
        out_shape=jax.ShapeDtypeStruct((t, d), F32),
        compiler_params=_params("parallel"),
    )(o2d, x2d, w_out, g, b)


def _router_kernel(x_ref, wh_ref, wl_ref, b_ref, e_ref, g_ref, r_ref, cnt_ref, carry_ref):
    i = pl.program_id(0)

    @pl.when(i == 0)
    def _():
        carry_ref[...] = jnp.zeros_like(carry_ref)

    x = x_ref[...]
    xh = x.astype(BF16)
    xl = (x - xh.astype(F32)).astype(BF16)
    wh = wh_ref[...]
    nt = functools.partial(lax.dot_general, dimension_numbers=_NT, preferred_element_type=F32)
    logits = nt(wh, xh) + nt(wl_ref[...], xh) + nt(wh, xl) + b_ref[:, 0:1]
    n_e, tm = logits.shape
    eid = lax.broadcasted_iota(I32, (n_e, tm), 0)

    work = logits
    tops, ids, sels = [], [], []
    for _ in range(TOP_K):
        m = jnp.max(work, axis=0, keepdims=True)
        idx = jnp.min(jnp.where(work == m, eid, n_e), axis=0, keepdims=True)
        sel = eid == idx
        work = jnp.where(sel, -jnp.inf, work)
        tops.append(m)
        ids.append(idx)
        sels.append(sel)
    ex = [jnp.exp(m - tops[0]) for m in tops]
    denom = ex[0] + ex[1] + ex[2] + ex[3]
    e_ref[...] = jnp.concatenate(ids, axis=0)
    g_ref[...] = jnp.concatenate([v / denom for v in ex], axis=0)

    chosen = jnp.where(sels[0] | sels[1] | sels[2] | sels[3], 1.0, 0.0)
    r = lax.broadcasted_iota(I32, (tm, tm), 0)
    c = lax.broadcasted_iota(I32, (tm, tm), 1)
    before = jnp.where(r < c, 1.0, 0.0).astype(BF16)
    seen = _bdot(chosen.astype(BF16), before) + carry_ref[:, 0:1]
    r_ref[...] = jnp.concatenate(
        [jnp.sum(jnp.where(s, seen, 0.0), axis=0, keepdims=True) for s in sels], axis=0).astype(I32)
    total = carry_ref[...] + jnp.sum(chosen, axis=1, keepdims=True)
    carry_ref[...] = total
    cnt_ref[...] = total


def _route(x2d, w_hi, w_lo, bias):
    t, d = x2d.shape
    n_e = w_hi.shape[0]
    tok = lambda i: (0, i)
    fixed = lambda i: (0, 0)
    return pl.pallas_call(
        _router_kernel,
        grid=(t // ROUTE_TILE,),
        in_specs=[pl.BlockSpec((ROUTE_TILE, d), lambda i: (i, 0)),
                  pl.BlockSpec((n_e, d), fixed), pl.BlockSpec((n_e, d), fixed),
                  pl.BlockSpec((n_e, LANES), fixed)],
        out_specs=[pl.BlockSpec((TOP_K, ROUTE_TILE), tok), pl.BlockSpec((TOP_K, ROUTE_TILE), tok),
                   pl.BlockSpec((TOP_K, ROUTE_TILE), tok), pl.BlockSpec((n_e, LANES), fixed)],
        out_shape=[jax.ShapeDtypeStruct((TOP_K, t), I32), jax.ShapeDtypeStruct((TOP_K, t), F32),
                   jax.ShapeDtypeStruct((TOP_K, t), I32), jax.ShapeDtypeStruct((n_e, LANES), F32)],
        scratch_shapes=[pltpu.VMEM((n_e, LANES), F32)],
        compiler_params=_params("arbitrary"),
    )(x2d, w_hi, w_lo, bias)


def _dest_kernel(pstart_ref, e_ref, r_ref, d_ref):
    e = e_ref[...]
    base = jnp.zeros(e.shape, I32)
    for j in range(N_EXPERTS):
        base = jnp.where(e == j, pstart_ref[j], base)
    d_ref[...] = base + r_ref[...]


def _destinations(pstart, top_e, rank):
    k, t = top_e.shape
    blk = min(t, 4096)
    spec = pl.BlockSpec((k, blk), lambda i: (0, i))
    return pl.pallas_call(
        _dest_kernel,
        grid=(t // blk,),
        in_specs=[pl.BlockSpec(memory_space=pltpu.SMEM), spec, spec],
        out_specs=spec,
        out_shape=jax.ShapeDtypeStruct((k, t), I32),
        compiler_params=_params("parallel"),
    )(pstart, top_e, rank)


def _scatter_kernel(dest_ref, tail_ref, x_ref, xs_hbm, zero_buf, sem, zsem):
    tile = x_ref.shape[0]

    @pl.when(pl.program_id(0) == 0)
    def _():
        zero_buf[...] = jnp.zeros_like(zero_buf)

        def zero_block(j):
            return pltpu.make_async_copy(
                zero_buf, xs_hbm.at[pl.ds(pl.multiple_of(j * EXPERT_TILE, EXPERT_TILE), EXPERT_TILE)], zsem)

        n_blocks = xs_hbm.shape[0] // EXPERT_TILE
        first_unused = tail_ref[N_EXPERTS]
        for step in ("start", "wait"):
            for e in range(N_EXPERTS):
                @pl.when(tail_ref[e] >= 0)
                def _(e=e, step=step):
                    getattr(zero_block(tail_ref[e]), step)()

            @pl.loop(first_unused, n_blocks)
            def _(j, step=step):
                getattr(zero_block(j), step)()

    def issue(t, _):
        for k in range(TOP_K):
            pltpu.make_async_copy(x_ref.at[pl.ds(t, 1)],
                                  xs_hbm.at[pl.ds(dest_ref[k, t], 1)], sem).start(priority=k % 2)
        return 0

    lax.fori_loop(0, tile, issue, 0, unroll=ISSUE_UNROLL)
    rows = xs_hbm.at[pl.ds(0, TOP_K * tile)]
    pltpu.make_async_copy(rows, rows, sem).wait()


def _dispatch(x2d, dest, tails, n_rows):
    t, d = x2d.shape
    tile = min(MOVE_TILE, t)
    return pl.pallas_call(
        _scatter_kernel,
        grid=(t // tile,),
        in_specs=[pl.BlockSpec((TOP_K, tile), lambda i: (0, i), memory_space=pltpu.SMEM),
                  pl.BlockSpec(memory_space=pltpu.SMEM),
                  pl.BlockSpec((tile, d), lambda i: (i, 0))],
        out_specs=pl.BlockSpec(memory_space=pl.ANY),
        out_shape=jax.ShapeDtypeStruct((n_rows, d), x2d.dtype),
        scratch_shapes=[pltpu.VMEM((EXPERT_TILE, d), x2d.dtype), pltpu.SemaphoreType.DMA(()),
                        pltpu.SemaphoreType.DMA(())],
        compiler_params=_params("arbitrary"),
    )(dest, tails, x2d)


def _expert_kernel(be_ref, nu_ref, xs_ref, wg_ref, bg_ref, wu_ref, bu_ref, wd_ref, bd_ref, y_ref,
                   wg_bf, wu_bf, wd_bf):
    i = pl.program_id(0)
    used = i < nu_ref[0]
    new_expert = jnp.logical_or(i == 0, be_ref[i] != be_ref[jnp.maximum(i - 1, 0)])

    @pl.when(jnp.logical_and(used, new_expert))
    def _():
        wg_bf[...] = wg_ref[...].astype(BF16)
        wu_bf[...] = wu_ref[...].astype(BF16)
        wd_bf[...] = wd_ref[...].astype(BF16)

    @pl.when(used)
    def _():
        xb = xs_ref[...].astype(BF16)
        hg = jnp.minimum(_bdot(xb, wg_bf[...]) + bg_ref[...], SWIGLU_LIMIT)
        hu = jnp.clip(_bdot(xb, wu_bf[...]) + bu_ref[...], -SWIGLU_LIMIT, SWIGLU_LIMIT)
        a = hg * jax.nn.sigmoid(SWIGLU_ALPHA * hg) * (hu + 1.0)
        y_ref[...] = _bdot(a.astype(BF16), wd_bf[...]) + bd_ref[...]

    @pl.when(jnp.logical_not(used))
    def _():
        y_ref[...] = jnp.zeros_like(y_ref)


def _experts(layer, block_e, n_used, xs, w_gate, b_gate, w_up, b_up, w_down, b_down):
    n_rows, d = xs.shape
    f = w_gate.shape[3]
    rows = lambda i, be, nu: (i, 0)
    used_rows = lambda i, be, nu: (jnp.minimum(i, jnp.maximum(nu[0] - 1, 0)), 0)
    wsel = lambda i, be, nu: (layer, be[i], 0, 0)
    grid_spec = pltpu.PrefetchScalarGridSpec(
        num_scalar_prefetch=2,
        grid=(n_rows // EXPERT_TILE,),
        in_specs=[pl.BlockSpec((EXPERT_TILE, d), used_rows),
                  pl.BlockSpec((None, None, d, f), wsel), pl.BlockSpec((None, None, 1, f), wsel),
                  pl.BlockSpec((None, None, d, f), wsel), pl.BlockSpec((None, None, 1, f), wsel),
                  pl.BlockSpec((None, None, f, d), wsel), pl.BlockSpec((None, None, 1, d), wsel)],
        out_specs=pl.BlockSpec((EXPERT_TILE, d), rows),
        scratch_shapes=[pltpu.VMEM((d, f), BF16), pltpu.VMEM((d, f), BF16), pltpu.VMEM((f, d), BF16)],
    )
    return pl.pallas_call(
        _expert_kernel,
        grid_spec=grid_spec,
        out_shape=jax.ShapeDtypeStruct((n_rows, d), F32),
        compiler_params=pltpu.CompilerParams(dimension_semantics=("arbitrary",),
                                             vmem_limit_bytes=EXPERT_VMEM_LIMIT),
    )(block_e, n_used, xs, w_gate, b_gate, w_up, b_up, w_down, b_down)


def _combine_kernel(dest_ref, next_ref, y_hbm, x_ref, g_ref, p_ref, wp_ref, wg_ref, bg_ref, lg_ref, lb_ref,
                    o_ref, buf_a, buf_b, sem_a, sem_b, *, alpha):
    tm = x_ref.shape[0]
    i = pl.program_id(0)
    last = pl.num_programs(0) - 1

    def gather(idx_ref, buf, sem):
        for t in range(tm):
            for k in range(TOP_K):
                pltpu.make_async_copy(y_hbm.at[pl.ds(idx_ref[k, t], 1)],
                                      buf.at[k, pl.ds(t, 1)], sem).start(priority=k % 2)

    def wait(buf, sem):
        pltpu.make_async_copy(buf, buf, sem).wait()

    @pl.when(i == 0)
    def _():
        gather(dest_ref, buf_a, sem_a)

    def step(buf, sem, other_buf, other_sem):
        wait(buf, sem)
        gather(next_ref, other_buf, other_sem)
        gates = jnp.concatenate([g_ref[...], jnp.zeros((LANES - TOP_K, tm), F32)], axis=0).T
        embed = _bdot(p_ref[...].astype(BF16), wp_ref[...])
        ffn = buf[0] * gates[:, 0:1]
        for k in range(1, TOP_K):
            ffn = ffn + buf[k] * gates[:, k:k + 1]
        x2 = _layer_norm(alpha * x_ref[...] + ffn, lg_ref[...], lb_ref[...])
        gate = jax.nn.sigmoid(_bdot(x2.astype(BF16), wg_ref[...]) + bg_ref[...])
        o_ref[...] = x2 + embed * gate

        @pl.when(i == last)
        def _():
            wait(other_buf, other_sem)

    @pl.when(i % 2 == 0)
    def _():
        step(buf_a, sem_a, buf_b, sem_b)

    @pl.when(i % 2 == 1)
    def _():
        step(buf_b, sem_b, buf_a, sem_a)


def _combine(dest, y, x2d, gates, p2d, w_proj, w_gate, b_gate, ln_g, ln_b, alpha):
    t, d = x2d.shape
    pd = p2d.shape[1]
    tile = min(COMBINE_TILE, t)
    n = t // tile
    row = lambda i: (i, 0)
    tok = lambda i: (0, i)
    nxt = lambda i: (0, (i + 1) % n)
    fixed = lambda i: (0, 0)
    return pl.pallas_call(
        functools.partial(_combine_kernel, alpha=alpha),
        grid=(n,),
        in_specs=[pl.BlockSpec((TOP_K, tile), tok, memory_space=pltpu.SMEM),
                  pl.BlockSpec((TOP_K, tile), nxt, memory_space=pltpu.SMEM),
                  pl.BlockSpec(memory_space=pl.ANY),
                  pl.BlockSpec((tile, d), row),
                  pl.BlockSpec((TOP_K, tile), tok),
                  pl.BlockSpec((tile, pd), row),
                  pl.BlockSpec((pd, d), fixed), pl.BlockSpec((d, d), fixed),
                  pl.BlockSpec((1, d), fixed), pl.BlockSpec((1, d), fixed), pl.BlockSpec((1, d), fixed)],
        out_specs=pl.BlockSpec((tile, d), row),
        out_shape=jax.ShapeDtypeStruct((t, d), F32),
        scratch_shapes=[pltpu.VMEM((TOP_K, tile, d), F32), pltpu.VMEM((TOP_K, tile, d), F32),
                        pltpu.SemaphoreType.DMA(()), pltpu.SemaphoreType.DMA(())],
        compiler_params=_params("arbitrary"),
    )(dest, dest, y, x2d, gates, p2d, w_proj, w_gate, b_gate, ln_g, ln_b)


def _moe(layer, x2d, router_w, router_b, w_gate, b_gate, w_up, b_up, w_down, b_down):
    t, d = x2d.shape
    w_t = router_w.T
    w_hi = w_t.astype(BF16)
    w_lo = (w_t - w_hi.astype(F32)).astype(BF16)
    bias = jnp.broadcast_to(router_b[:, None], (N_EXPERTS, LANES))
    top_e, gates, rank, counts = _route(x2d, w_hi, w_lo, bias)

    n_blocks = -(-t * TOP_K // EXPERT_TILE) + N_EXPERTS
    cnt = counts[:, 0].astype(I32)
    padded = (cnt + EXPERT_TILE - 1) // EXPERT_TILE * EXPERT_TILE
    pend = jnp.cumsum(padded)
    pstart = pend - padded
    block_start = jnp.arange(n_blocks, dtype=I32) * EXPERT_TILE
    block_e = jnp.minimum(jnp.sum(pend[None, :] <= block_start[:, None], axis=1), N_EXPERTS - 1).astype(I32)
    n_used = (pend[-1:] // EXPERT_TILE).astype(I32)

    dest = _destinations(pstart.astype(I32), top_e, rank)
    n_rows = n_blocks * EXPERT_TILE
    tails = jnp.where(cnt > 0, pend // EXPERT_TILE - 1, -1)
    tails = jnp.concatenate([tails, n_used]).astype(I32)
    xs = _dispatch(x2d, dest, tails, n_rows)
    y = _experts(layer, block_e, n_used, xs, w_gate, b_gate[:, :, None, :], w_up, b_up[:, :, None, :],
                 w_down, b_down[:, :, None, :])
    return dest, y, gates


def kernel(x, p, conv_w_in, conv_w, conv_w_out, sb_w_in, sb_w_out, fox_w_in, fox_b_f, fox_w_out, ln1_g, ln1_b, ln2_g, ln2_b, router_w, router_b, exp_w_gate, exp_b_gate, exp_w_up, exp_b_up, exp_w_down, exp_b_down, ple_w_proj, ple_w_gate, ple_b_gate):
    bsz, s, d = x.shape
    depth = p.shape[0]
    t = bsz * s
    alpha = float((2 * depth) ** 0.25)
    assert d == N_HEADS * HEAD_DIM and s % ROW_TILE == 0 and s % ATT_K == 0 and ATT_K % ATT_Q == 0
    assert t % ROUTE_TILE == 0 and t % min(MOVE_TILE, t) == 0 and t % min(COMBINE_TILE, t) == 0

    xf = x.reshape(t, d)
    for i in range(depth):
        kind, j = i % N_MIXERS, i // N_MIXERS
        if kind == 0:
            mixed = _conv_mix(xf, conv_w_in[j].astype(BF16), conv_w[j], s)
            w_out = conv_w_out[j]
        elif kind == 1:
            qkv = _project(xf, sb_w_in[j].astype(BF16), BF16).reshape(bsz, s, 3 * d)
            mixed = _attention(_sb_kernel, qkv, [], []).reshape(t, d)
            w_out = sb_w_out[j]
        else:
            w_in = fox_w_in[j]
            w_f = jnp.pad(w_in[:, 3 * d:], ((0, 0), (0, LANES - N_HEADS))).astype(BF16)
            b_f = jnp.pad(fox_b_f[j], (0, LANES - N_HEADS))[None, :]
            qkv, cum = _fox_project(xf, w_in[:, :3 * d].astype(BF16), w_f, b_f, s)
            cum = cum[:, :N_HEADS].reshape(bsz, s, N_HEADS)
            cum_t = jnp.swapaxes(cum, 1, 2)
            mixed = _attention(
                _fox_kernel, qkv.reshape(bsz, s, 3 * d), [cum, cum_t],
                [pl.BlockSpec((1, ATT_Q, N_HEADS), lambda bi, hi, qi: (bi, qi, 0)),
                 pl.BlockSpec((1, N_HEADS, s), lambda bi, hi, qi: (bi, 0, 0))]).reshape(t, d)
            w_out = fox_w_out[j]
        x1 = _out_ln(mixed, xf, w_out.astype(BF16), ln1_g[i][None, :], ln1_b[i][None, :], alpha)
        dest, y, gates = _moe(i, x1, router_w[i], router_b[i], exp_w_gate, exp_b_gate, exp_w_up,
                              exp_b_up, exp_w_down, exp_b_down)
        xf = _combine(dest, y, x1, gates, p[i].reshape(t, -1), ple_w_proj[i].astype(BF16),
                      ple_w_gate[i].astype(BF16), ple_b_gate[i][None, :], ln2_g[i][None, :],
                      ln2_b[i][None, :], alpha)
    return xf.reshape(bsz, s, d)
```

```python
import functools

import jax
import jax.numpy as jnp
from jax import lax
from jax.experimental import pallas as pl
from jax.experimental.pallas import tpu as pltpu

F32 = jnp.float32
BF16 = jnp.bfloat16
I32 = jnp.int32

N_HEADS = 16
HEAD_DIM = 64
N_MIXERS = 3
N_EXPERTS = 32
TOP_K = 4
SWIGLU_LIMIT = 7.0
SWIGLU_ALPHA = 1.702
LN_EPS = 1e-5
LANES = 128
VMEM_LIMIT = 48 * 1024 * 1024
EXPERT_VMEM_LIMIT = 58 * 1024 * 1024

ROW_TILE = 512
ATT_Q = 256
ATT_K = 256
ATT_UNROLL = 4
FOX_DEAD_LOG = -104.0
SB_DEAD_LOG = -104.0
ROUTE_TILE = 512
EXPERT_TILE = 512
MOVE_TILE = 1024
COMBINE_TILE = 256
ISSUE_UNROLL = 4

_NT = (((1,), (1,)), ((), ()))


def _params(*sem):
    return pltpu.CompilerParams(dimension_semantics=sem, vmem_limit_bytes=VMEM_LIMIT)


def _bdot(a, b):
    return jnp.dot(a, b, preferred_element_type=F32)


def _split3(v):
    hi = v.astype(BF16)
    r = v - hi.astype(F32)
    mid = r.astype(BF16)
    lo = (r - mid.astype(F32)).astype(BF16)
    return hi, mid, lo


def _layer_norm(h, g, b):
    mu = jnp.mean(h, axis=-1, keepdims=True)
    c = h - mu
    var = jnp.mean(c * c, axis=-1, keepdims=True)
    return c * lax.rsqrt(var + LN_EPS) * g + b


def _proj_kernel(x_ref, w_ref, o_ref):
    o_ref[...] = _bdot(x_ref[...].astype(BF16), w_ref[...]).astype(o_ref.dtype)


def _project(x2d, w_bf16, out_dtype):
    t, d = x2d.shape
    n = w_bf16.shape[1]
    return pl.pallas_call(
        _proj_kernel,
        grid=(t // ROW_TILE,),
        in_specs=[pl.BlockSpec((ROW_TILE, d), lambda i: (i, 0)),
                  pl.BlockSpec((d, n), lambda i: (0, 0))],
        out_specs=pl.BlockSpec((ROW_TILE, n), lambda i: (i, 0)),
        out_shape=jax.ShapeDtypeStruct((t, n), out_dtype),
        compiler_params=_params("parallel"),
    )(x2d, w_bf16)


def _fox_proj_kernel(x_ref, w_ref, wf_ref, bf_ref, qkv_ref, cum_ref, carry_ref, *, tiles_per_seq):
    i = pl.program_id(0)

    @pl.when(i % tiles_per_seq == 0)
    def _():
        carry_ref[...] = jnp.zeros_like(carry_ref)

    xb = x_ref[...].astype(BF16)
    qkv_ref[...] = _bdot(xb, w_ref[...]).astype(qkv_ref.dtype)
    f = _bdot(xb, wf_ref[...]) + bf_ref[...]
    log_f = jnp.minimum(f, 0.0) - jnp.log1p(jnp.exp(-jnp.abs(f)))
    tm = f.shape[0]
    r = lax.broadcasted_iota(I32, (tm, tm), 0)
    c = lax.broadcasted_iota(I32, (tm, tm), 1)
    tri = jnp.where(c <= r, 1.0, 0.0).astype(BF16)
    hi, mid, lo = _split3(log_f)
    cum = _bdot(tri, hi) + _bdot(tri, mid) + _bdot(tri, lo) + carry_ref[...]
    cum_ref[...] = cum
    carry_ref[...] = cum[tm - 1:tm, :]


def _fox_project(x2d, w_qkv, w_f, b_f, seq):
    t, d = x2d.shape
    n = w_qkv.shape[1]
    return pl.pallas_call(
        functools.partial(_fox_proj_kernel, tiles_per_seq=seq // ROW_TILE),
        grid=(t // ROW_TILE,),
        in_specs=[pl.BlockSpec((ROW_TILE, d), lambda i: (i, 0)),
                  pl.BlockSpec((d, n), lambda i: (0, 0)),
                  pl.BlockSpec((d, LANES), lambda i: (0, 0)),
                  pl.BlockSpec((1, LANES), lambda i: (0, 0))],
        out_specs=[pl.BlockSpec((ROW_TILE, n), lambda i: (i, 0)),
                   pl.BlockSpec((ROW_TILE, LANES), lambda i: (i, 0))],
        out_shape=[jax.ShapeDtypeStruct((t, n), BF16),
                   jax.ShapeDtypeStruct((t, LANES), F32)],
        scratch_shapes=[pltpu.VMEM((1, LANES), F32)],
        compiler_params=_params("arbitrary"),
    )(x2d, w_qkv, w_f, b_f)


def _conv_kernel(x_ref, w_ref, cw_ref, o_ref, tail_ref, *, tiles_per_seq):
    i = pl.program_id(0)

    @pl.when(i % tiles_per_seq == 0)
    def _():
        tail_ref[...] = jnp.zeros_like(tail_ref)

    d = o_ref.shape[1]
    proj = _bdot(x_ref[...].astype(BF16), w_ref[...])
    gate_b = proj[:, :d]
    u = proj[:, d:2 * d] * proj[:, 2 * d:]
    tm = u.shape[0]
    row = lax.broadcasted_iota(I32, u.shape, 0)
    prev1 = jnp.where(row == 0, tail_ref[1:2, :], pltpu.roll(u, 1, 0))
    prev2 = jnp.where(row == 0, tail_ref[0:1, :],
                      jnp.where(row == 1, tail_ref[1:2, :], pltpu.roll(u, 2, 0)))
    conv = cw_ref[0:1, :] * prev2 + cw_ref[1:2, :] * prev1 + cw_ref[2:3, :] * u
    o_ref[...] = (gate_b * conv).astype(o_ref.dtype)
    tail_ref[...] = u[tm - 2:tm, :]


def _conv_mix(x2d, w_in, conv_w, seq):
    t, d = x2d.shape
    return pl.pallas_call(
        functools.partial(_conv_kernel, tiles_per_seq=seq // ROW_TILE),
        grid=(t // ROW_TILE,),
        in_specs=[pl.BlockSpec((ROW_TILE, d), lambda i: (i, 0)),
                  pl.BlockSpec((d, 3 * d), lambda i: (0, 0)),
                  pl.BlockSpec(conv_w.shape, lambda i: (0, 0))],
        out_specs=pl.BlockSpec((ROW_TILE, d), lambda i: (i, 0)),
        out_shape=jax.ShapeDtypeStruct((t, d), BF16),
        scratch_shapes=[pltpu.VMEM((2, d), F32)],
        compiler_params=_params("arbitrary"),
    )(x2d, w_in, conv_w)


def _head_views(q):
    lane = lax.broadcasted_iota(I32, q.shape, 1)
    zero = jnp.zeros_like(q)
    return [jnp.where((lane // HEAD_DIM) == h, q, zero) for h in range(LANES // HEAD_DIM)]


def _merge_heads(accs):
    lane = lax.broadcasted_iota(I32, accs[0].shape, 1)
    out = accs[-1]
    for h in range(len(accs) - 2, -1, -1):
        out = jnp.where((lane // HEAD_DIM) == h, accs[h], out)
    return out


def _sb_kernel(q_ref, k_ref, v_ref, o_ref):
    tq = q_ref.shape[1]
    tk = ATT_K
    qi = pl.program_id(2)
    kd = (qi * tq) // tk
    qs = _head_views(q_ref[0] * jnp.asarray(HEAD_DIM ** -0.5, BF16))
    row = lax.broadcasted_iota(I32, (tq, tk), 0)
    col = lax.broadcasted_iota(I32, (tq, tk), 1)
    kr = lax.broadcasted_iota(I32, (tk, tk), 0)
    kc = lax.broadcasted_iota(I32, (tk, tk), 1)
    neg_later = jnp.where(kr > kc, -1.0, 0.0).astype(BF16)
    strict = col < row + (qi * tq - kd * tk)

    def blocks(kbs, carry, masked=()):
        ks, vs, zs, sps, spms, boths = [], [], [], [], [], []
        for kb in kbs:
            start = pl.multiple_of(kb * tk, tk)
            ks.append(k_ref[0, pl.ds(start, tk), :])
            vs.append(v_ref[0, pl.ds(start, tk), :])
        for k in ks:
            zs.append([lax.dot_general(qh, k, _NT, preferred_element_type=F32) for qh in qs])
        for i, zb in enumerate(zs):
            spb, spmb, bothb = [], [], []
            for z in zb:
                sp = jnp.maximum(z, 0.0) + jnp.log(1.0 + jnp.exp(-jnp.abs(z)))
                spm = jnp.where(strict, sp, 0.0) if i in masked else sp
                hi = spm.astype(BF16)
                lo = (spm - hi.astype(F32)).astype(BF16)
                spb.append(sp)
                spmb.append(spm)
                bothb.append(_bdot(jnp.concatenate([hi, lo], axis=0), neg_later))
            sps.append(spb)
            spms.append(spmb)
            boths.append(bothb)
        accs = [acc for acc, _ in carry]
        runs = [run for _, run in carry]
        ws = []
        for i, (zb, spb, spmb, bothb) in enumerate(zip(zs, sps, spms, boths)):
            wb = []
            for h in range(len(qs)):
                inner = bothb[h][:tq] + bothb[h][tq:]
                w = jnp.exp((zb[h] - spb[h]) + (inner + runs[h]))
                if i in masked:
                    w = jnp.where(strict, w, 0.0)
                wb.append(w.astype(BF16))
                runs[h] = runs[h] + (inner[:, 0:1] - spmb[h][:, 0:1])
            ws.append(wb)
        for wb, v in zip(ws, vs):
            for h in range(len(qs)):
                accs[h] = accs[h] + _bdot(wb[h], v)
        return tuple(zip(accs, runs))

    def alive(carry):
        worst = carry[0][1]
        for _, run in carry[1:]:
            worst = jnp.maximum(worst, run)
        return jnp.max(worst) > SB_DEAD_LOG

    init = tuple((jnp.zeros((tq, LANES), F32), jnp.zeros((tq, 1), F32)) for _ in qs)
    carry = lax.cond(kd > 0, lambda c: blocks([kd, kd - 1], c, masked=(0,)),
                     lambda c: blocks([kd], c, masked=(0,)), init)
    left = jnp.maximum(kd - 1, 0)
    u = ATT_UNROLL

    def group(state):
        j, c, _ = state
        c = blocks([left - 1 - u * j - i for i in range(u)], c)
        return j + 1, c, alive(c)

    _, carry, live = lax.while_loop(lambda st: jnp.logical_and(st[0] < left // u, st[2]), group,
                                    (jnp.int32(0), carry, alive(carry)))
    rem = jnp.where(live, left % u, 0)
    carry = lax.fori_loop(0, rem // 2, lambda j, c: blocks([rem - 1, rem - 2], c), carry)
    carry = lax.fori_loop(0, rem % 2, lambda j, c: blocks([0], c), carry)
    o_ref[0] = _merge_heads([acc for acc, _ in carry]).astype(o_ref.dtype)


def _fox_kernel(q_ref, k_ref, v_ref, cq_ref, ck_ref, o_ref, kmax_ref):
    tq = q_ref.shape[1]
    tk = ATT_K
    qi = pl.program_id(2)
    hp = pl.program_id(1)
    kd = (qi * tq) // tk
    n_h = LANES // HEAD_DIM
    qs = _head_views(q_ref[0] * jnp.asarray(HEAD_DIM ** -0.5, BF16))
    row = lax.broadcasted_iota(I32, (tq, tk), 0)
    col = lax.broadcasted_iota(I32, (tq, tk), 1)
    causal = col <= row + (qi * tq - kd * tk)
    cq_all = cq_ref[0]
    head_lane = lax.broadcasted_iota(I32, cq_all.shape, 1)
    cqs = [jnp.sum(jnp.where(head_lane == hp * n_h + h, cq_all, 0.0), axis=1, keepdims=True)
           for h in range(n_h)]

    def blocks(kbs, carry, diagonal):
        zs, vs = [], []
        for kb in kbs:
            start = pl.multiple_of(kb * tk, tk)
            k = k_ref[0, pl.ds(start, tk), :]
            vs.append(v_ref[0, pl.ds(start, tk), :])
            zb = []
            for h, (qh, cq) in enumerate(zip(qs, cqs)):
                ck = ck_ref[0, pl.ds(hp * n_h + h, 1), pl.ds(start, tk)]
                z = lax.dot_general(qh, k, _NT, preferred_element_type=F32) + (cq - ck)
                zb.append(jnp.where(causal, z, -jnp.inf) if diagonal else z)
            zs.append(zb)
        accs = [acc for acc, _, _ in carry]
        ms = [m for _, m, _ in carry]
        ls = [l for _, _, l in carry]
        ps, alphas = [], []
        for zb in zs:
            pb, ab = [], []
            for h, z in enumerate(zb):
                m_new = jnp.maximum(ms[h], jnp.max(z, axis=1, keepdims=True))
                alpha = jnp.exp(ms[h] - m_new)
                p = jnp.exp(z - m_new)
                ls[h] = ls[h] * alpha + jnp.sum(p, axis=1, keepdims=True)
                ms[h] = m_new
                pb.append(p.astype(BF16))
                ab.append(alpha)
            ps.append(pb)
            alphas.append(ab)
        for pb, ab, v in zip(ps, alphas, vs):
            for h in range(n_h):
                accs[h] = accs[h] * ab[h] + _bdot(pb[h], v)
        return tuple(zip(accs, ms, ls))

    init = tuple((jnp.zeros((tq, LANES), F32), jnp.full((tq, 1), -jnp.inf, F32), jnp.zeros((tq, 1), F32))
                 for _ in qs)
    carry = blocks([kd], init, True)

    @pl.when(qi == 0)
    def _():
        kmax_ref[...] = jnp.max(jnp.abs(k_ref[0].astype(F32)), axis=0, keepdims=True)

    k_abs = kmax_ref[...]
    k_lane = lax.broadcasted_iota(I32, k_abs.shape, 1) // HEAD_DIM
    reach = [jnp.sum(jnp.abs(qh.astype(F32)), axis=1, keepdims=True)
             * jnp.max(jnp.where(k_lane == h, k_abs, 0.0), axis=1, keepdims=True) + cqs[h]
             for h, qh in enumerate(qs)]

    def alive(carry, kb):
        worst = None
        for h, (_, m, _) in enumerate(carry):
            tail = ck_ref[0, pl.ds(hp * n_h + h, 1), pl.ds(pl.multiple_of(kb * tk, tk), tk)]
            ck_end = jnp.min(tail, axis=1, keepdims=True)
            gap = jnp.max(reach[h] - m, axis=0, keepdims=True) - ck_end
            worst = gap if worst is None else jnp.maximum(worst, gap)
        return jnp.max(worst) > FOX_DEAD_LOG

    u = ATT_UNROLL

    def group(state):
        j, c, _ = state
        c = blocks([kd - 1 - u * j - i for i in range(u)], c, False)
        return j + 1, c, alive(c, jnp.maximum(kd - 1 - u * (j + 1), 0))

    _, carry, live = lax.while_loop(lambda st: jnp.logical_and(st[0] < kd // u, st[2]), group,
                                    (jnp.int32(0), carry, alive(carry, jnp.maximum(kd - 1, 0))))
    rem = jnp.where(live, kd % u, 0)
    carry = lax.fori_loop(0, rem // 2, lambda j, c: blocks([rem - 1, rem - 2], c, False), carry)
    carry = lax.fori_loop(0, rem % 2, lambda j, c: blocks([0], c, False), carry)
    o_ref[0] = _merge_heads([acc / l for acc, _, l in carry]).astype(o_ref.dtype)


def _attention(kernel_fn, qkv, extra, extra_specs, scratch=()):
    b, s, d3 = qkv.shape
    d = d3 // 3
    pairs = d // LANES
    qspec = pl.BlockSpec((1, ATT_Q, LANES), lambda bi, hi, qi: (bi, qi, hi))
    kspec = pl.BlockSpec((1, s, LANES), lambda bi, hi, qi: (bi, 0, pairs + hi))
    vspec = pl.BlockSpec((1, s, LANES), lambda bi, hi, qi: (bi, 0, 2 * pairs + hi))
    return pl.pallas_call(
        kernel_fn,
        grid=(b, pairs, s // ATT_Q),
        in_specs=[qspec, kspec, vspec] + extra_specs,
        out_specs=pl.BlockSpec((1, ATT_Q, LANES), lambda bi, hi, qi: (bi, qi, hi)),
        out_shape=jax.ShapeDtypeStruct((b, s, d), BF16),
        scratch_shapes=list(scratch),
        compiler_params=_params("parallel", "parallel", "arbitrary"),
    )(qkv, qkv, qkv, *extra)


def _out_ln_kernel(o_ref, x_ref, w_ref, g_ref, b_ref, y_ref, *, alpha):
    h = alpha * x_ref[...] + _bdot(o_ref[...], w_ref[...])
    y_ref[...] = _layer_norm(h, g_ref[...], b_ref[...])


def _out_ln(o2d, x2d, w_out, g, b, alpha):
    t, d = x2d.shape
    row = lambda i: (i, 0)
    fixed = lambda i: (0, 0)
    return pl.pallas_call(
        functools.partial(_out_ln_kernel, alpha=alpha),
        grid=(t // ROW_TILE,),
        in_specs=[pl.BlockSpec((ROW_TILE, d), row), pl.BlockSpec((ROW_TILE, d), row),
                  pl.BlockSpec((d, d), fixed), pl.BlockSpec((1, d), fixed), pl.BlockSpec((1, d), fixed)],
        out_specs=pl.BlockSpec((ROW_TILE, d), row),
        out_shape=jax.ShapeDtypeStruct((t, d), F32),
        compiler_params=_params("parallel"),
    )(o2d, x2d, w_out, g, b)


def _router_kernel(x_ref, wh_ref, wl_ref, b_ref, e_ref, g_ref, r_ref, cnt_ref, carry_ref):
    i = pl.program_id(0)

    @pl.when(i == 0)
    def _():
        carry_ref[...] = jnp.zeros_like(carry_ref)

    x = x_ref[...]
    xh = x.astype(BF16)
    xl = (x - xh.astype(F32)).astype(BF16)
    wh = wh_ref[...]
    nt = functools.partial(lax.dot_general, dimension_numbers=_NT, preferred_element_type=F32)
    logits = nt(wh, xh) + nt(wl_ref[...], xh) + nt(wh, xl) + b_ref[:, 0:1]
    n_e, tm = logits.shape
    eid = lax.broadcasted_iota(I32, (n_e, tm), 0)

    work = logits
    tops, ids, sels = [], [], []
    for _ in range(TOP_K):
        m = jnp.max(work, axis=0, keepdims=True)
        idx = jnp.min(jnp.where(work == m, eid, n_e), axis=0, keepdims=True)
        sel = eid == idx
        work = jnp.where(sel, -jnp.inf, work)
        tops.append(m)
        ids.append(idx)
        sels.append(sel)
    ex = [jnp.exp(m - tops[0]) for m in tops]
    denom = ex[0] + ex[1] + ex[2] + ex[3]
    e_ref[...] = jnp.concatenate(ids, axis=0)
    g_ref[...] = jnp.concatenate([v / denom for v in ex], axis=0)

    chosen = jnp.where(sels[0] | sels[1] | sels[2] | sels[3], 1.0, 0.0)
    r = lax.broadcasted_iota(I32, (tm, tm), 0)
    c = lax.broadcasted_iota(I32, (tm, tm), 1)
    before = jnp.where(r < c, 1.0, 0.0).astype(BF16)
    seen = _bdot(chosen.astype(BF16), before) + carry_ref[:, 0:1]
    r_ref[...] = jnp.concatenate(
        [jnp.sum(jnp.where(s, seen, 0.0), axis=0, keepdims=True) for s in sels], axis=0).astype(I32)
    total = carry_ref[...] + jnp.sum(chosen, axis=1, keepdims=True)
    carry_ref[...] = total
    cnt_ref[...] = total


def _route(x2d, w_hi, w_lo, bias):
    t, d = x2d.shape
    n_e = w_hi.shape[0]
    tok = lambda i: (0, i)
    fixed = lambda i: (0, 0)
    return pl.pallas_call(
        _router_kernel,
        grid=(t // ROUTE_TILE,),
        in_specs=[pl.BlockSpec((ROUTE_TILE, d), lambda i: (i, 0)),
                  pl.BlockSpec((n_e, d), fixed), pl.BlockSpec((n_e, d), fixed),
                  pl.BlockSpec((n_e, LANES), fixed)],
        out_specs=[pl.BlockSpec((TOP_K, ROUTE_TILE), tok), pl.BlockSpec((TOP_K, ROUTE_TILE), tok),
                   pl.BlockSpec((TOP_K, ROUTE_TILE), tok), pl.BlockSpec((n_e, LANES), fixed)],
        out_shape=[jax.ShapeDtypeStruct((TOP_K, t), I32), jax.ShapeDtypeStruct((TOP_K, t), F32),
                   jax.ShapeDtypeStruct((TOP_K, t), I32), jax.ShapeDtypeStruct((n_e, LANES), F32)],
        scratch_shapes=[pltpu.VMEM((n_e, LANES), F32)],
        compiler_params=_params("arbitrary"),
    )(x2d, w_hi, w_lo, bias)


def _dest_kernel(pstart_ref, e_ref, r_ref, d_ref):
    e = e_ref[...]
    base = jnp.zeros(e.shape, I32)
    for j in range(N_EXPERTS):
        base = jnp.where(e == j, pstart_ref[j], base)
    d_ref[...] = base + r_ref[...]


def _destinations(pstart, top_e, rank):
    k, t = top_e.shape
    blk = min(t, 4096)
    spec = pl.BlockSpec((k, blk), lambda i: (0, i))
    return pl.pallas_call(
        _dest_kernel,
        grid=(t // blk,),
        in_specs=[pl.BlockSpec(memory_space=pltpu.SMEM), spec, spec],
        out_specs=spec,
        out_shape=jax.ShapeDtypeStruct((k, t), I32),
        compiler_params=_params("parallel"),
    )(pstart, top_e, rank)


def _scatter_kernel(dest_ref, tail_ref, x_ref, xs_hbm, zero_buf, sem, zsem):
    tile = x_ref.shape[0]

    @pl.when(pl.program_id(0) == 0)
    def _():
        zero_buf[...] = jnp.zeros_like(zero_buf)

        def zero_block(j):
            return pltpu.make_async_copy(
                zero_buf, xs_hbm.at[pl.ds(pl.multiple_of(j * EXPERT_TILE, EXPERT_TILE), EXPERT_TILE)], zsem)

        n_blocks = xs_hbm.shape[0] // EXPERT_TILE
        first_unused = tail_ref[N_EXPERTS]
        for step in ("start", "wait"):
            for e in range(N_EXPERTS):
                @pl.when(tail_ref[e] >= 0)
                def _(e=e, step=step):
                    getattr(zero_block(tail_ref[e]), step)()

            @pl.loop(first_unused, n_blocks)
            def _(j, step=step):
                getattr(zero_block(j), step)()

    def issue(t, _):
        for k in range(TOP_K):
            pltpu.make_async_copy(x_ref.at[pl.ds(t, 1)],
                                  xs_hbm.at[pl.ds(dest_ref[k, t], 1)], sem).start(priority=k % 2)
        return 0

    lax.fori_loop(0, tile, issue, 0, unroll=ISSUE_UNROLL)
    rows = xs_hbm.at[pl.ds(0, TOP_K * tile)]
    pltpu.make_async_copy(rows, rows, sem).wait()


def _dispatch(x2d, dest, tails, n_rows):
    t, d = x2d.shape
    tile = min(MOVE_TILE, t)
    return pl.pallas_call(
        _scatter_kernel,
        grid=(t // tile,),
        in_specs=[pl.BlockSpec((TOP_K, tile), lambda i: (0, i), memory_space=pltpu.SMEM),
                  pl.BlockSpec(memory_space=pltpu.SMEM),
                  pl.BlockSpec((tile, d), lambda i: (i, 0))],
        out_specs=pl.BlockSpec(memory_space=pl.ANY),
        out_shape=jax.ShapeDtypeStruct((n_rows, d), x2d.dtype),
        scratch_shapes=[pltpu.VMEM((EXPERT_TILE, d), x2d.dtype), pltpu.SemaphoreType.DMA(()),
                        pltpu.SemaphoreType.DMA(())],
        compiler_params=_params("arbitrary"),
    )(dest, tails, x2d)


def _expert_kernel(be_ref, nu_ref, xs_ref, wg_ref, bg_ref, wu_ref, bu_ref, wd_ref, bd_ref, y_ref,
                   wg_bf, wu_bf, wd_bf):
    i = pl.program_id(0)
    used = i < nu_ref[0]
    new_expert = jnp.logical_or(i == 0, be_ref[i] != be_ref[jnp.maximum(i - 1, 0)])

    @pl.when(jnp.logical_and(used, new_expert))
    def _():
        wg_bf[...] = wg_ref[...].astype(BF16)
        wu_bf[...] = wu_ref[...].astype(BF16)
        wd_bf[...] = wd_ref[...].astype(BF16)

    @pl.when(used)
    def _():
        xb = xs_ref[...].astype(BF16)
        hg = jnp.minimum(_bdot(xb, wg_bf[...]) + bg_ref[...], SWIGLU_LIMIT)
        hu = jnp.clip(_bdot(xb, wu_bf[...]) + bu_ref[...], -SWIGLU_LIMIT, SWIGLU_LIMIT)
        a = hg * jax.nn.sigmoid(SWIGLU_ALPHA * hg) * (hu + 1.0)
        y_ref[...] = _bdot(a.astype(BF16), wd_bf[...]) + bd_ref[...]

    @pl.when(jnp.logical_not(used))
    def _():
        y_ref[...] = jnp.zeros_like(y_ref)


def _experts(layer, block_e, n_used, xs, w_gate, b_gate, w_up, b_up, w_down, b_down):
    n_rows, d = xs.shape
    f = w_gate.shape[3]
    rows = lambda i, be, nu: (i, 0)
    used_rows = lambda i, be, nu: (jnp.minimum(i, jnp.maximum(nu[0] - 1, 0)), 0)
    wsel = lambda i, be, nu: (layer, be[i], 0, 0)
    grid_spec = pltpu.PrefetchScalarGridSpec(
        num_scalar_prefetch=2,
        grid=(n_rows // EXPERT_TILE,),
        in_specs=[pl.BlockSpec((EXPERT_TILE, d), used_rows),
                  pl.BlockSpec((None, None, d, f), wsel), pl.BlockSpec((None, None, 1, f), wsel),
                  pl.BlockSpec((None, None, d, f), wsel), pl.BlockSpec((None, None, 1, f), wsel),
                  pl.BlockSpec((None, None, f, d), wsel), pl.BlockSpec((None, None, 1, d), wsel)],
        out_specs=pl.BlockSpec((EXPERT_TILE, d), rows),
        scratch_shapes=[pltpu.VMEM((d, f), BF16), pltpu.VMEM((d, f), BF16), pltpu.VMEM((f, d), BF16)],
    )
    return pl.pallas_call(
        _expert_kernel,
        grid_spec=grid_spec,
        out_shape=jax.ShapeDtypeStruct((n_rows, d), F32),
        compiler_params=pltpu.CompilerParams(dimension_semantics=("arbitrary",),
                                             vmem_limit_bytes=EXPERT_VMEM_LIMIT),
    )(block_e, n_used, xs, w_gate, b_gate, w_up, b_up, w_down, b_down)


def _combine_kernel(dest_ref, next_ref, y_hbm, x_ref, g_ref, p_ref, wp_ref, wg_ref, bg_ref, lg_ref, lb_ref,
                    o_ref, buf_a, buf_b, sem_a, sem_b, *, alpha):
    tm = x_ref.shape[0]
    i = pl.program_id(0)
    last = pl.num_programs(0) - 1

    def gather(idx_ref, buf, sem):
        for t in range(tm):
            for k in range(TOP_K):
                pltpu.make_async_copy(y_hbm.at[pl.ds(idx_ref[k, t], 1)],
                                      buf.at[k, pl.ds(t, 1)], sem).start(priority=k % 2)

    def wait(buf, sem):
        pltpu.make_async_copy(buf, buf, sem).wait()

    @pl.when(i == 0)
    def _():
        gather(dest_ref, buf_a, sem_a)

    def step(buf, sem, other_buf, other_sem):
        wait(buf, sem)
        gather(next_ref, other_buf, other_sem)
        gates = jnp.concatenate([g_ref[...], jnp.zeros((LANES - TOP_K, tm), F32)], axis=0).T
        embed = _bdot(p_ref[...].astype(BF16), wp_ref[...])
        ffn = buf[0] * gates[:, 0:1]
        for k in range(1, TOP_K):
            ffn = ffn + buf[k] * gates[:, k:k + 1]
        x2 = _layer_norm(alpha * x_ref[...] + ffn, lg_ref[...], lb_ref[...])
        gate = jax.nn.sigmoid(_bdot(x2.astype(BF16), wg_ref[...]) + bg_ref[...])
        o_ref[...] = x2 + embed * gate

        @pl.when(i == last)
        def _():
            wait(other_buf, other_sem)

    @pl.when(i % 2 == 0)
    def _():
        step(buf_a, sem_a, buf_b, sem_b)

    @pl.when(i % 2 == 1)
    def _():
        step(buf_b, sem_b, buf_a, sem_a)


def _combine(dest, y, x2d, gates, p2d, w_proj, w_gate, b_gate, ln_g, ln_b, alpha):
    t, d = x2d.shape
    pd = p2d.shape[1]
    tile = min(COMBINE_TILE, t)
    n = t // tile
    row = lambda i: (i, 0)
    tok = lambda i: (0, i)
    nxt = lambda i: (0, (i + 1) % n)
    fixed = lambda i: (0, 0)
    return pl.pallas_call(
        functools.partial(_combine_kernel, alpha=alpha),
        grid=(n,),
        in_specs=[pl.BlockSpec((TOP_K, tile), tok, memory_space=pltpu.SMEM),
                  pl.BlockSpec((TOP_K, tile), nxt, memory_space=pltpu.SMEM),
                  pl.BlockSpec(memory_space=pl.ANY),
                  pl.BlockSpec((tile, d), row),
                  pl.BlockSpec((TOP_K, tile), tok),
                  pl.BlockSpec((tile, pd), row),
                  pl.BlockSpec((pd, d), fixed), pl.BlockSpec((d, d), fixed),
                  pl.BlockSpec((1, d), fixed), pl.BlockSpec((1, d), fixed), pl.BlockSpec((1, d), fixed)],
        out_specs=pl.BlockSpec((tile, d), row),
        out_shape=jax.ShapeDtypeStruct((t, d), F32),
        scratch_shapes=[pltpu.VMEM((TOP_K, tile, d), F32), pltpu.VMEM((TOP_K, tile, d), F32),
                        pltpu.SemaphoreType.DMA(()), pltpu.SemaphoreType.DMA(())],
        compiler_params=_params("arbitrary"),
    )(dest, dest, y, x2d, gates, p2d, w_proj, w_gate, b_gate, ln_g, ln_b)


def _moe(layer, x2d, router_w, router_b, w_gate, b_gate, w_up, b_up, w_down, b_down):
    t, d = x2d.shape
    w_t = router_w.T
    w_hi = w_t.astype(BF16)
    w_lo = (w_t - w_hi.astype(F32)).astype(BF16)
    bias = jnp.broadcast_to(router_b[:, None], (N_EXPERTS, LANES))
    top_e, gates, rank, counts = _route(x2d, w_hi, w_lo, bias)

    n_blocks = -(-t * TOP_K // EXPERT_TILE) + N_EXPERTS
    cnt = counts[:, 0].astype(I32)
    padded = (cnt + EXPERT_TILE - 1) // EXPERT_TILE * EXPERT_TILE
    pend = jnp.cumsum(padded)
    pstart = pend - padded
    block_start = jnp.arange(n_blocks, dtype=I32) * EXPERT_TILE
    block_e = jnp.minimum(jnp.sum(pend[None, :] <= block_start[:, None], axis=1), N_EXPERTS - 1).astype(I32)
    n_used = (pend[-1:] // EXPERT_TILE).astype(I32)

    dest = _destinations(pstart.astype(I32), top_e, rank)
    n_rows = n_blocks * EXPERT_TILE
    tails = jnp.where(cnt > 0, pend // EXPERT_TILE - 1, -1)
    tails = jnp.concatenate([tails, n_used]).astype(I32)
    xs = _dispatch(x2d, dest, tails, n_rows)
    y = _experts(layer, block_e, n_used, xs, w_gate, b_gate[:, :, None, :], w_up, b_up[:, :, None, :],
                 w_down, b_down[:, :, None, :])
    return dest, y, gates


def kernel(x, p, conv_w_in, conv_w, conv_w_out, sb_w_in, sb_w_out, fox_w_in, fox_b_f, fox_w_out, ln1_g, ln1_b, ln2_g, ln2_b, router_w, router_b, exp_w_gate, exp_b_gate, exp_w_up, exp_b_up, exp_w_down, exp_b_down, ple_w_proj, ple_w_gate, ple_b_gate):
    bsz, s, d = x.shape
    depth = p.shape[0]
    t = bsz * s
    alpha = float((2 * depth) ** 0.25)
    assert d == N_HEADS * HEAD_DIM and s % ROW_TILE == 0 and s % ATT_K == 0 and ATT_K % ATT_Q == 0
    assert t % ROUTE_TILE == 0 and t % min(MOVE_TILE, t) == 0 and t % min(COMBINE_TILE, t) == 0

    xf = x.reshape(t, d)
    for i in range(depth):
        kind, j = i % N_MIXERS, i // N_MIXERS
        if kind == 0:
            mixed = _conv_mix(xf, conv_w_in[j].astype(BF16), conv_w[j], s)
            w_out = conv_w_out[j]
        elif kind == 1:
            qkv = _project(xf, sb_w_in[j].astype(BF16), BF16).reshape(bsz, s, 3 * d)
            mixed = _attention(_sb_kernel, qkv, [], []).reshape(t, d)
            w_out = sb_w_out[j]
        else:
            w_in = fox_w_in[j]
            w_f = jnp.pad(w_in[:, 3 * d:], ((0, 0), (0, LANES - N_HEADS))).astype(BF16)
            b_f = jnp.pad(fox_b_f[j], (0, LANES - N_HEADS))[None, :]
            qkv, cum = _fox_project(xf, w_in[:, :3 * d].astype(BF16), w_f, b_f, s)
            cum = cum[:, :N_HEADS].reshape(bsz, s, N_HEADS)
            cum_t = jnp.swapaxes(cum, 1, 2)
            mixed = _attention(
                _fox_kernel, qkv.reshape(bsz, s, 3 * d), [cum, cum_t],
                [pl.BlockSpec((1, ATT_Q, N_HEADS), lambda bi, hi, qi: (bi, qi, 0)),
                 pl.BlockSpec((1, N_HEADS, s), lambda bi, hi, qi: (bi, 0, 0))],
                scratch=[pltpu.VMEM((1, LANES), F32)]).reshape(t, d)
            w_out = fox_w_out[j]
        x1 = _out_ln(mixed, xf, w_out.astype(BF16), ln1_g[i][None, :], ln1_b[i][None, :], alpha)
        dest, y, gates = _moe(i, x1, router_w[i], router_b[i], exp_w_gate, exp_b_gate, exp_w_up,
                              exp_b_up, exp_w_down, exp_b_down)
        xf = _combine(dest, y, x1, gates, p[i].reshape(t, -1), ple_w_proj[i].astype(BF16),
                      ple_w_gate[i].astype(BF16), ple_b_gate[i][None, :], ln2_g[i][None, :],
                      ln2_b[i][None, :], alpha)
    return xf.reshape(bsz, s, d)
```
